```python
import jax, jax.numpy as jnp
from jax import lax
import numpy as np

D_MODEL = 1024
BATCH = 8
SEQ = 4096
DEPTH = 1

D_MIX = D_MODEL
ATTN_HEAD_DIM = 64
ATTN_HEADS = (D_MIX // 2) // ATTN_HEAD_DIM
ATTN_KV_HEADS = 2
ATTN_WIDTH = ATTN_HEADS * ATTN_HEAD_DIM
ATTN_KV_WIDTH = ATTN_KV_HEADS * ATTN_HEAD_DIM
WINDOW = 128
BLOCK = 128
ROPE_THETA = 10000.0

GLA_HEADS = 4
GLA_WIDTH = D_MIX - ATTN_WIDTH
GLA_DV = GLA_WIDTH // GLA_HEADS
GLA_DK = GLA_DV // 2
GLA_QK_WIDTH = GLA_HEADS * GLA_DK
GLA_GATE_RANK = 16
GLA_GATE_TAU = 16.0
GLA_CHUNK = 64

D_FF = 2816
PLE_DIM = 256
EPS = 1e-6

IN_SIZES = (ATTN_WIDTH, ATTN_KV_WIDTH, ATTN_KV_WIDTH,
            GLA_QK_WIDTH, GLA_QK_WIDTH, GLA_WIDTH, GLA_GATE_RANK, GLA_WIDTH)
D_IN = sum(IN_SIZES)

kernel_name = "hymba_swa_sink_gla_macaron_ple"


def rmsnorm(x, g):
    xf = x.astype(jnp.float32)
    y = xf * lax.rsqrt(jnp.mean(xf * xf, axis=-1, keepdims=True) + EPS)
    return (y * g.astype(jnp.float32)).astype(x.dtype)


def swiglu(x, w_gate, w_up, w_down):
    return (jax.nn.silu(x @ w_gate) * (x @ w_up)) @ w_down


def rope(x, pos):
    dh = x.shape[-1]
    inv_freq = ROPE_THETA ** (-jnp.arange(0, dh, 2, dtype=jnp.float32) / dh)
    ang = pos.astype(jnp.float32)[:, None] * inv_freq[None, :]
    cos = jnp.cos(ang)[None, :, None, :]
    sin = jnp.sin(ang)[None, :, None, :]
    xf = x.astype(jnp.float32)
    x1, x2 = xf[..., : dh // 2], xf[..., dh // 2:]
    out = jnp.concatenate([x1 * cos - x2 * sin, x2 * cos + x1 * sin], axis=-1)
    return out.astype(x.dtype)


def split_cols(z):
    points = np.cumsum(IN_SIZES)[:-1].tolist()
    return jnp.split(z, points, axis=-1)


def sliding_window_attention(q, k, v, sinks):
    B, S, H, Dh = q.shape
    KVH = k.shape[2]
    G = H // KVH
    NB = S // BLOCK
    qb = q.reshape(B, NB, BLOCK, KVH, G, Dh)

    def band(t):
        tb = t.reshape(B, NB, BLOCK, KVH, Dh)
        prev = jnp.pad(tb[:, :-1], ((0, 0), (1, 0), (0, 0), (0, 0), (0, 0)))
        return jnp.concatenate([prev, tb], axis=2)

    kb, vb = band(k), band(v)
    scores = jnp.einsum('bnqhgd,bnkhd->bnhgqk', qb, kb).astype(jnp.float32) * (Dh ** -0.5)
    qi = jnp.arange(BLOCK)[:, None]
    kj = jnp.arange(2 * BLOCK)[None, :]
    rel = kj - BLOCK - qi
    in_band = (rel <= 0) & (rel > -WINDOW)
    kpos = jnp.arange(NB)[:, None, None] * BLOCK + kj[None] - BLOCK
    valid = in_band[None] & (kpos >= 0)
    scores = jnp.where(valid[None, :, None, None], scores, -jnp.inf)
    sink = sinks.astype(jnp.float32).reshape(KVH, G)[None, None, :, :, None, None]
    m = jnp.maximum(jnp.max(scores, axis=-1, keepdims=True), sink)
    e = jnp.exp(scores - m)
    denom = jnp.sum(e, axis=-1, keepdims=True) + jnp.exp(sink - m)
    probs = (e / denom).astype(v.dtype)
    out = jnp.einsum('bnhgqk,bnkhd->bnqhgd', probs, vb)
    return out.reshape(B, S, H * Dh)


def gla_chunked(q, k, v, log_alpha):
    B, S, H, DK = q.shape
    DV = v.shape[-1]
    C = GLA_CHUNK
    NC = S // C

    def chunks(t):
        return t.astype(jnp.float32).reshape(B, NC, C, H, t.shape[-1]).transpose(1, 0, 3, 2, 4)

    qc = chunks(q) * (DK ** -0.5)
    kc, vc, gc = chunks(k), chunks(v), chunks(log_alpha)
    causal = jnp.tril(jnp.ones((C, C), dtype=bool))[None, None, :, :, None]

    def step(state, inp):
        qi, ki, vi, gi = inp
        b = jnp.cumsum(gi, axis=2)
        b_last = b[:, :, -1:, :]
        diff = b[:, :, :, None, :] - b[:, :, None, :, :]
        decay = jnp.exp(jnp.where(causal, diff, -jnp.inf))
        attn = jnp.einsum('bhid,bhjd,bhijd->bhij', qi, ki, decay)
        o = (jnp.einsum('bhij,bhjv->bhiv', attn, vi)
             + jnp.einsum('bhid,bhdv->bhiv', qi * jnp.exp(b), state))
        new_state = (jnp.swapaxes(jnp.exp(b_last), 2, 3) * state
                     + jnp.einsum('bhjd,bhjv->bhdv', ki * jnp.exp(b_last - b), vi))
        return new_state, o

    s0 = jnp.zeros((B, H, DK, DV), jnp.float32)
    _, o = lax.scan(step, s0, (qc, kc, vc, gc))
    return o.transpose(1, 0, 3, 2, 4).reshape(B, S, H, DV).astype(v.dtype)


def setup_inputs(seed: int = 0) -> dict:
    key = jax.random.key(seed)
    ks = jax.random.split(key, 24)
    L, D, F = DEPTH, D_MODEL, D_FF

    def w(k, shape, fan_in):
        return jax.random.normal(k, shape, jnp.float32) * (fan_in ** -0.5)

    def gain(k, shape):
        return 1.0 + 0.05 * jax.random.normal(k, shape, jnp.float32)

    return {
        "x": jax.random.normal(ks[0], (BATCH, SEQ, D), jnp.float32),
        "p": jax.random.normal(ks[1], (L, BATCH, SEQ, PLE_DIM), jnp.float32),
        "ffn1_norm": gain(ks[2], (L, D)),
        "ffn1_w_gate": w(ks[3], (L, D, F), D),
        "ffn1_w_up": w(ks[4], (L, D, F), D),
        "ffn1_w_down": w(ks[5], (L, F, D), F),
        "mix_norm": gain(ks[6], (L, D)),
        "w_in": w(ks[7], (L, D, D_IN), D),
        "q_norm": gain(ks[8], (L, ATTN_HEAD_DIM)),
        "k_norm": gain(ks[9], (L, ATTN_HEAD_DIM)),
        "attn_sinks": 0.5 * jax.random.normal(ks[10], (L, ATTN_HEADS), jnp.float32),
        "gla_w_gate_up": w(ks[11], (L, GLA_GATE_RANK, GLA_QK_WIDTH), GLA_GATE_RANK),
        "gla_b_gate": 0.1 * jax.random.normal(ks[12], (L, GLA_QK_WIDTH), jnp.float32),
        "gla_out_norm": gain(ks[13], (L, GLA_DV)),
        "w_out": w(ks[14], (L, D_MIX, D), D_MIX),
        "ffn2_norm": gain(ks[15], (L, D)),
        "ffn2_w_gate": w(ks[16], (L, D, F), D),
        "ffn2_w_up": w(ks[17], (L, D, F), D),
        "ffn2_w_down": w(ks[18], (L, F, D), F),
        "ple_gate_norm": gain(ks[19], (L, D)),
        "ple_w_gate": w(ks[20], (L, D, D), D),
        "ple_w_proj": w(ks[21], (L, PLE_DIM, D), PLE_DIM),
        "ple_norm": gain(ks[22], (L, D)),
    }


def reference(x, p, ffn1_norm, ffn1_w_gate, ffn1_w_up, ffn1_w_down, mix_norm, w_in,
              q_norm, k_norm, attn_sinks, gla_w_gate_up, gla_b_gate, gla_out_norm, w_out,
              ffn2_norm, ffn2_w_gate, ffn2_w_up, ffn2_w_down,
              ple_gate_norm, ple_w_gate, ple_w_proj, ple_norm):
    B, S, _ = x.shape
    pos = jnp.arange(S)
    h = x
    for i in range(DEPTH):
        h = h + 0.5 * swiglu(rmsnorm(h, ffn1_norm[i]), ffn1_w_gate[i], ffn1_w_up[i], ffn1_w_down[i])

        n = rmsnorm(h, mix_norm[i])
        aq, ak, av, gq, gk, gv, glr, gog = split_cols(n @ w_in[i])

        aq = rope(rmsnorm(aq.reshape(B, S, ATTN_HEADS, ATTN_HEAD_DIM), q_norm[i]), pos)
        ak = rope(rmsnorm(ak.reshape(B, S, ATTN_KV_HEADS, ATTN_HEAD_DIM), k_norm[i]), pos)
        av = av.reshape(B, S, ATTN_KV_HEADS, ATTN_HEAD_DIM)
        attn_o = sliding_window_attention(aq, ak, av, attn_sinks[i])

        log_alpha = jax.nn.log_sigmoid(
            (glr @ gla_w_gate_up[i] + gla_b_gate[i]).astype(jnp.float32)) / GLA_GATE_TAU
        gla_o = gla_chunked(gq.reshape(B, S, GLA_HEADS, GLA_DK),
                            gk.reshape(B, S, GLA_HEADS, GLA_DK),
                            gv.reshape(B, S, GLA_HEADS, GLA_DV),
                            log_alpha.reshape(B, S, GLA_HEADS, GLA_DK))
        gla_o = rmsnorm(gla_o, gla_out_norm[i]) * jax.nn.silu(gog.reshape(B, S, GLA_HEADS, GLA_DV))
        gla_o = gla_o.reshape(B, S, GLA_WIDTH)

        h = h + jnp.concatenate([attn_o, gla_o], axis=-1) @ w_out[i]

        h = h + 0.5 * swiglu(rmsnorm(h, ffn2_norm[i]), ffn2_w_gate[i], ffn2_w_up[i], ffn2_w_down[i])

        gate = jax.nn.sigmoid(rmsnorm(h, ple_gate_norm[i]) @ ple_w_gate[i])
        emb = rmsnorm(p[i] @ ple_w_proj[i], ple_norm[i])
        h = h + gate * emb
    return h
```

```python
import functools

import numpy as np
import jax
import jax.numpy as jnp
from jax import lax
from jax.experimental import pallas as pl
from jax.experimental.pallas import tpu as pltpu

F32 = jnp.float32
BF16 = jnp.bfloat16

D_MODEL = 1024
D_FF = 2816
PLE_DIM = 256
EPS = 1e-6

ATTN_HEAD_DIM = 64
ATTN_HEADS = 8
ATTN_KV_HEADS = 2
ATTN_WIDTH = ATTN_HEADS * ATTN_HEAD_DIM
ATTN_KV_WIDTH = ATTN_KV_HEADS * ATTN_HEAD_DIM
ROPE_THETA = 10000.0

GLA_HEADS = 4
GLA_DV = 128
GLA_DK = 64
GLA_QK_WIDTH = GLA_HEADS * GLA_DK
GLA_WIDTH = GLA_HEADS * GLA_DV
GLA_GATE_RANK = 16
GLA_GATE_TAU = 16.0

LANES = 128
TILE = 128
ROW_TILE = 512
FF_CHUNK = D_FF // 2
VMEM_LIMIT = 56 * 1024 * 1024

Z_AQ, Z_AK, Z_AV = 0, 512, 640
Z_GQ, Z_GK, Z_GV, Z_GOG, Z_GLR = 768, 1024, 1280, 1792, 2304
Z_WIDTH = Z_GLR + LANES

N_LEVELS = 7
DIAG_LEVEL = N_LEVELS


def _rmsnorm(x, gain):
    return x * lax.rsqrt(jnp.mean(x * x, axis=-1, keepdims=True) + EPS) * gain


def _swiglu(nb, wg_ref, wu_ref, wd_ref):
    acc = None
    for c in range(D_FF // FF_CHUNK):
        sl = slice(c * FF_CHUNK, (c + 1) * FF_CHUNK)
        g = jnp.dot(nb, wg_ref[:, sl], preferred_element_type=F32)
        u = jnp.dot(nb, wu_ref[:, sl], preferred_element_type=F32)
        a = (g * jax.nn.sigmoid(g) * u).astype(BF16)
        d = jnp.dot(a, wd_ref[sl, :], preferred_element_type=F32)
        acc = d if acc is None else acc + d
    return acc


def _ffn1_inproj_kernel(x_ref, n1_ref, wg_ref, wu_ref, wd_ref, n2_ref, win_ref, h_ref, z_ref):
    x = x_ref[...]
    h = x + 0.5 * _swiglu(_rmsnorm(x, n1_ref[...]).astype(BF16), wg_ref, wu_ref, wd_ref)
    h_ref[...] = h
    n = _rmsnorm(h, n2_ref[...]).astype(BF16)
    z_ref[...] = jnp.dot(n, win_ref[...], preferred_element_type=F32)


def _outproj_ffn2_ple_kernel(mix_ref, h1_ref, p_ref, wo_ref, n1_ref, wg_ref, wu_ref, wd_ref,
                             gn_ref, wpg_ref, wpp_ref, pn_ref, out_ref):
    h2 = h1_ref[...] + jnp.dot(mix_ref[...], wo_ref[...], preferred_element_type=F32)
    h3 = h2 + 0.5 * _swiglu(_rmsnorm(h2, n1_ref[...]).astype(BF16), wg_ref, wu_ref, wd_ref)
    gate = jax.nn.sigmoid(jnp.dot(_rmsnorm(h3, gn_ref[...]).astype(BF16), wpg_ref[...],
                                  preferred_element_type=F32))
    emb = _rmsnorm(jnp.dot(p_ref[...].astype(BF16), wpp_ref[...], preferred_element_type=F32), pn_ref[...])
    out_ref[...] = h3 + gate * emb


def _dot_nt(a, b):
    return lax.dot_general(a, b, (((1,), (1,)), ((), ())), preferred_element_type=F32)


def _dot_tn(a, b):
    return lax.dot_general(a, b, (((0,), (0,)), ((), ())), preferred_element_type=F32)


def _mixer_kernel(sinks_ref, z_ref, cos_ref, sin_ref, qn_ref, kn_ref, wup_ref, bg_ref, gon_ref,
                  mstack_ref, lid_ref, mix_ref, kprev_ref, vprev_ref, state_ref):
    s = pl.program_id(1)

    @pl.when(s == 0)
    def _():
        kprev_ref[...] = jnp.zeros_like(kprev_ref)
        vprev_ref[...] = jnp.zeros_like(vprev_ref)
        state_ref[...] = jnp.zeros_like(state_ref)

    lane = lax.broadcasted_iota(jnp.int32, (TILE, LANES), 1)
    row = lax.broadcasted_iota(jnp.int32, (TILE, LANES), 0)
    slot0 = lane < ATTN_HEAD_DIM
    first_half = (lane % ATTN_HEAD_DIM) < (ATTN_HEAD_DIM // 2)
    upper = lane > row
    cos = cos_ref[...]
    sin = sin_ref[...]

    def qk_norm_rope(xp, gain):
        sq = xp * xp
        s0 = jnp.sum(jnp.where(slot0, sq, 0.0), axis=-1, keepdims=True)
        s1 = jnp.sum(jnp.where(slot0, 0.0, sq), axis=-1, keepdims=True)
        ms = jnp.where(slot0, s0, s1) * (1.0 / ATTN_HEAD_DIM)
        y = xp * lax.rsqrt(ms + EPS) * gain
        partner = jnp.where(first_half, pltpu.roll(y, LANES - ATTN_HEAD_DIM // 2, 1),
                            pltpu.roll(y, ATTN_HEAD_DIM // 2, 1))
        return y * cos + partner * sin

    kb = qk_norm_rope(z_ref[:, Z_AK:Z_AK + LANES], kn_ref[...]).astype(BF16)
    vb = z_ref[:, Z_AV:Z_AV + LANES].astype(BF16)
    k_band = jnp.concatenate([kprev_ref[...], kb], axis=0)
    v_band = jnp.concatenate([vprev_ref[...], vb], axis=0)
    kprev_ref[...] = kb
    vprev_ref[...] = vb
    upper2 = jnp.concatenate([upper, upper], axis=0)
    row2 = lax.broadcasted_iota(jnp.int32, (2 * TILE, 1), 0)
    prev_ok = s > 0
    for p in range(ATTN_HEADS // 2):
        qp = qk_norm_rope(z_ref[:, Z_AQ + p * LANES:Z_AQ + (p + 1) * LANES], qn_ref[...]) * (ATTN_HEAD_DIM ** -0.5)
        q2 = jnp.concatenate([jnp.where(slot0, qp, 0.0), jnp.where(slot0, 0.0, qp)], axis=0).astype(BF16)
        sc = _dot_nt(q2, k_band)
        s_prev = jnp.where(prev_ok, sc[:, :TILE], -jnp.inf)
        sc = jnp.where(upper2, s_prev, sc[:, TILE:])
        sink = jnp.where(row2 < TILE, sinks_ref[p], sinks_ref[p + ATTN_HEADS // 2])
        m = jnp.maximum(jnp.max(sc, axis=-1, keepdims=True), sink)
        e = jnp.exp(sc - m)
        denom = jnp.sum(e, axis=-1, keepdims=True) + jnp.exp(sink - m)
        probs = e * (1.0 / denom)
        pb = jnp.concatenate([jnp.where(upper2, probs, 0.0), jnp.where(upper2, 0.0, probs)], axis=1).astype(BF16)
        o2 = jnp.dot(pb, v_band, preferred_element_type=F32)
        mix_ref[:, p * LANES:(p + 1) * LANES] = jnp.where(slot0, o2[:TILE], o2[TILE:]).astype(BF16)

    x = jnp.dot(z_ref[:, Z_GLR:Z_GLR + LANES].astype(BF16), wup_ref[...], preferred_element_type=F32) + bg_ref[...]
    g = (jnp.minimum(x, 0.0) - jnp.log1p(jnp.exp(-jnp.abs(x)))) * (1.0 / GLA_GATE_TAU)
    g_hi = g.astype(BF16)
    g_lo = (g - g_hi.astype(F32)).astype(BF16)
    ms = mstack_ref[...]
    decay = jnp.exp(jnp.dot(ms, g_hi, preferred_element_type=F32) + jnp.dot(ms, g_lo, preferred_element_type=F32))
    ones = jnp.ones((TILE, LANES), BF16)
    tile_decay = jnp.exp(_dot_tn(g_hi, ones) + _dot_tn(g_lo, ones))
    lid2 = lid_ref[...]
    for pp in range(GLA_HEADS // 2):
        sl = slice(pp * LANES, (pp + 1) * LANES)
        qp = z_ref[:, Z_GQ + pp * LANES:Z_GQ + (pp + 1) * LANES] * (GLA_DK ** -0.5)
        kp = z_ref[:, Z_GK + pp * LANES:Z_GK + (pp + 1) * LANES]
        q2 = jnp.concatenate([jnp.where(slot0, qp, 0.0), jnp.where(slot0, 0.0, qp)], axis=0)
        attn = _dot_nt(q2.astype(BF16), kp.astype(BF16))
        attn = jnp.where(lid2 == DIAG_LEVEL, attn, 0.0)
        for l in range(N_LEVELS):
            xl = decay[l * TILE:(l + 1) * TILE, sl]
            ql = (q2 * jnp.concatenate([xl, xl], axis=0)).astype(BF16)
            kl = (kp * xl).astype(BF16)
            attn = jnp.where(lid2 == l, _dot_nt(ql, kl), attn)
        xb = decay[N_LEVELS * TILE:(N_LEVELS + 1) * TILE, sl]
        xs = decay[(N_LEVELS + 1) * TILE:(N_LEVELS + 2) * TILE, sl]
        qe = (q2 * jnp.concatenate([xb, xb], axis=0)).astype(BF16)
        st = state_ref[pp * LANES:(pp + 1) * LANES, :]
        stb = st.astype(BF16)
        attn_b = attn.astype(BF16)
        v_wide = z_ref[:, Z_GV + 2 * pp * GLA_DV:Z_GV + 2 * (pp + 1) * GLA_DV].astype(BF16)
        for hh in range(2):
            h = 2 * pp + hh
            lhs = jnp.concatenate([attn_b[hh * TILE:(hh + 1) * TILE], qe[hh * TILE:(hh + 1) * TILE]], axis=1)
            rhs = jnp.concatenate([v_wide[:, hh * GLA_DV:(hh + 1) * GLA_DV], stb], axis=0)
            o = jnp.dot(lhs, rhs, preferred_element_type=F32)
            gog = z_ref[:, Z_GOG + h * GLA_DV:Z_GOG + (h + 1) * GLA_DV]
            o = _rmsnorm(o, gon_ref[...]) * (gog * jax.nn.sigmoid(gog))
            mix_ref[:, ATTN_WIDTH + h * GLA_DV:ATTN_WIDTH + (h + 1) * GLA_DV] = o.astype(BF16)
        upd = _dot_tn((kp * xs).astype(BF16), v_wide)
        td = tile_decay[pp * LANES:(pp + 1) * LANES, :]
        half = GLA_DK
        state_ref[pp * LANES:pp * LANES + half, :] = td[:half] * st[:half] + upd[:half, :GLA_DV]
        state_ref[pp * LANES + half:(pp + 1) * LANES, :] = td[half:] * st[half:] + upd[half:, GLA_DV:]


def _gla_constants():
    r = np.arange(TILE)[:, None]
    t = np.arange(TILE)[None, :]
    mats = []
    for l in range(N_LEVELS):
        size = (TILE // 2) >> l
        same = (r // size) == (t // size)
        odd = ((r // size) % 2) == 1
        mats.append(same & ((odd & (t <= r)) | (~odd & (t > r))))
    mats.append(t <= r)
    mats.append(t > r)
    mstack = np.concatenate(mats, axis=0).astype(np.float32)
    x = r ^ t
    top_bit = np.floor(np.log2(np.maximum(x, 1))).astype(np.int32)
    lid = np.where(t < r, (N_LEVELS - 1) - top_bit, np.where(t == r, DIAG_LEVEL, -1)).astype(np.int32)
    return jnp.asarray(mstack, BF16), jnp.asarray(np.concatenate([lid, lid], axis=0))


def _rope_tables(seq):
    dh = ATTN_HEAD_DIM
    inv_freq = ROPE_THETA ** (-jnp.arange(0, dh, 2, dtype=F32) / dh)
    ang = jnp.arange(seq).astype(F32)[:, None] * inv_freq[None, :]
    cos, sin = jnp.cos(ang), jnp.sin(ang)
    cos_t = jnp.tile(jnp.concatenate([cos, cos], axis=-1), (1, LANES // dh))
    sin_t = jnp.tile(jnp.concatenate([-sin, sin], axis=-1), (1, LANES // dh))
    return cos_t, sin_t


def _pair_perm():
    heads = np.stack([np.arange(4), np.arange(4) + 4], axis=1).reshape(-1)
    return (heads[:, None] * ATTN_HEAD_DIM + np.arange(ATTN_HEAD_DIM)[None, :]).reshape(-1)


def _prep_w_in(w_in):
    aq, ak, av, gq, gk, gv, glr, gog = jnp.split(
        w_in, np.cumsum([ATTN_WIDTH, ATTN_KV_WIDTH, ATTN_KV_WIDTH, GLA_QK_WIDTH, GLA_QK_WIDTH, GLA_WIDTH,
                         GLA_GATE_RANK]).tolist(), axis=-1)
    glr = jnp.pad(glr, ((0, 0), (0, LANES - GLA_GATE_RANK)))
    return jnp.concatenate([aq[:, _pair_perm()], ak, av, gq, gk, gv, gog, glr], axis=-1).astype(BF16)


def _resident(shape):
    return pl.BlockSpec(shape, lambda *_: (0,) * len(shape), pipeline_mode=pl.Buffered(1))


def _layer(h, p, ffn1_norm, ffn1_w_gate, ffn1_w_up, ffn1_w_down, mix_norm, w_in, q_norm, k_norm, attn_sinks,
           gla_w_gate_up, gla_b_gate, gla_out_norm, w_out, ffn2_norm, ffn2_w_gate, ffn2_w_up, ffn2_w_down,
           ple_gate_norm, ple_w_gate, ple_w_proj, ple_norm):
    batch, seq, d = h.shape
    tokens = batch * seq
    x2 = h.reshape(tokens, d)
    rows = lambda width: pl.BlockSpec((ROW_TILE, width), lambda i: (i, 0))
    vec = lambda v: v.reshape(1, -1).astype(F32)

    h1, z = pl.pallas_call(
        _ffn1_inproj_kernel,
        grid=(tokens // ROW_TILE,),
        in_specs=[rows(d), _resident((1, d)), _resident((d, D_FF)), _resident((d, D_FF)), _resident((D_FF, d)),
                  _resident((1, d)), _resident((d, Z_WIDTH))],
        out_specs=[rows(d), rows(Z_WIDTH)],
        out_shape=[jax.ShapeDtypeStruct((tokens, d), F32), jax.ShapeDtypeStruct((tokens, Z_WIDTH), F32)],
        compiler_params=pltpu.CompilerParams(dimension_semantics=("arbitrary",), vmem_limit_bytes=VMEM_LIMIT),
        name="ffn1_inproj",
    )(x2, vec(ffn1_norm), ffn1_w_gate.astype(BF16), ffn1_w_up.astype(BF16), ffn1_w_down.astype(BF16),
      vec(mix_norm), _prep_w_in(w_in))

    cos_t, sin_t = _rope_tables(seq)
    mstack, lid2 = _gla_constants()
    tiles = seq // TILE
    wup = jnp.pad(gla_w_gate_up, ((0, LANES - GLA_GATE_RANK), (0, 0))).astype(BF16)
    full = lambda shape: pl.BlockSpec(shape, lambda b, s: (0,) * len(shape))
    mix = pl.pallas_call(
        _mixer_kernel,
        grid=(batch, tiles),
        in_specs=[pl.BlockSpec(memory_space=pltpu.SMEM),
                  pl.BlockSpec((TILE, Z_WIDTH), lambda b, s: (b * tiles + s, 0)),
                  pl.BlockSpec((TILE, LANES), lambda b, s: (s, 0)),
                  pl.BlockSpec((TILE, LANES), lambda b, s: (s, 0)),
                  full((1, LANES)), full((1, LANES)), full((LANES, GLA_QK_WIDTH)), full((1, GLA_QK_WIDTH)),
                  full((1, GLA_DV)), full(((N_LEVELS + 2) * TILE, TILE)), full((2 * TILE, TILE))],
        out_specs=pl.BlockSpec((TILE, d), lambda b, s: (b * tiles + s, 0)),
        out_shape=jax.ShapeDtypeStruct((tokens, d), BF16),
        scratch_shapes=[pltpu.VMEM((TILE, LANES), BF16), pltpu.VMEM((TILE, LANES), BF16),
                        pltpu.VMEM((GLA_QK_WIDTH, GLA_DV), F32)],
        compiler_params=pltpu.CompilerParams(dimension_semantics=("arbitrary", "arbitrary")),
        name="mixer",
    )(attn_sinks.astype(F32), z, cos_t, sin_t, vec(jnp.tile(q_norm, 2)), vec(jnp.tile(k_norm, 2)), wup,
      vec(gla_b_gate), vec(gla_out_norm), mstack, lid2)

    wo = jnp.concatenate([w_out[:ATTN_WIDTH][_pair_perm()], w_out[ATTN_WIDTH:]], axis=0).astype(BF16)
    out = pl.pallas_call(
        _outproj_ffn2_ple_kernel,
        grid=(tokens // ROW_TILE,),
        in_specs=[rows(d), rows(d), rows(PLE_DIM), _resident((d, d)), _resident((1, d)), _resident((d, D_FF)),
                  _resident((d, D_FF)), _resident((D_FF, d)), _resident((1, d)), _resident((d, d)),
                  _resident((PLE_DIM, d)), _resident((1, d))],
        out_specs=rows(d),
        out_shape=jax.ShapeDtypeStruct((tokens, d), F32),
        compiler_params=pltpu.CompilerParams(dimension_semantics=("arbitrary",), vmem_limit_bytes=VMEM_LIMIT),
        name="outproj_ffn2_ple",
    )(mix, h1, p.reshape(tokens, PLE_DIM), wo, vec(ffn2_norm), ffn2_w_gate.astype(BF16), ffn2_w_up.astype(BF16),
      ffn2_w_down.astype(BF16), vec(ple_gate_norm), ple_w_gate.astype(BF16), ple_w_proj.astype(BF16),
      vec(ple_norm))
    return out.reshape(batch, seq, d)


@jax.jit
def kernel(x, p, ffn1_norm, ffn1_w_gate, ffn1_w_up, ffn1_w_down, mix_norm, w_in, q_norm, k_norm, attn_sinks,
           gla_w_gate_up, gla_b_gate, gla_out_norm, w_out, ffn2_norm, ffn2_w_gate, ffn2_w_up, ffn2_w_down,
           ple_gate_norm, ple_w_gate, ple_w_proj, ple_norm):
    h = x
    for i in range(p.shape[0]):
        h = _layer(h, p[i], ffn1_norm[i], ffn1_w_gate[i], ffn1_w_up[i], ffn1_w_down[i], mix_norm[i], w_in[i],
                   q_norm[i], k_norm[i], attn_sinks[i], gla_w_gate_up[i], gla_b_gate[i], gla_out_norm[i], w_out[i],
                   ffn2_norm[i], ffn2_w_gate[i], ffn2_w_up[i], ffn2_w_down[i], ple_gate_norm[i], ple_w_gate[i],
                   ple_w_proj[i], ple_norm[i])
    return h
```

```python
import functools

import numpy as np
import jax
import jax.numpy as jnp
from jax import lax
from jax.experimental import pallas as pl
from jax.experimental.pallas import tpu as pltpu

F32 = jnp.float32
BF16 = jnp.bfloat16

D_MODEL = 1024
D_FF = 2816
PLE_DIM = 256
EPS = 1e-6

ATTN_HEAD_DIM = 64
ATTN_HEADS = 8
ATTN_KV_HEADS = 2
ATTN_WIDTH = ATTN_HEADS * ATTN_HEAD_DIM
ATTN_KV_WIDTH = ATTN_KV_HEADS * ATTN_HEAD_DIM
ROPE_THETA = 10000.0

GLA_HEADS = 4
GLA_DV = 128
GLA_DK = 64
GLA_QK_WIDTH = GLA_HEADS * GLA_DK
GLA_WIDTH = GLA_HEADS * GLA_DV
GLA_GATE_RANK = 16
GLA_GATE_TAU = 16.0

LANES = 128
TILE = 128
SEQS_PER_STEP = 2
ROW_TILE = 512
FF_CHUNK = D_FF // 2
VMEM_LIMIT = 56 * 1024 * 1024

Z_AQ, Z_AK, Z_AV = 0, 512, 640
Z_GQ, Z_GK, Z_GV, Z_GOG, Z_GLR = 768, 1024, 1280, 1792, 2304
Z_WIDTH = Z_GLR + LANES

N_LEVELS = 7
DIAG_LEVEL = N_LEVELS


def _rmsnorm(x, gain):
    return x * lax.rsqrt(jnp.mean(x * x, axis=-1, keepdims=True) + EPS) * gain


def _swiglu(nb, wg_ref, wu_ref, wd_ref):
    acc = None
    for c in range(D_FF // FF_CHUNK):
        sl = slice(c * FF_CHUNK, (c + 1) * FF_CHUNK)
        g = jnp.dot(nb, wg_ref[:, sl], preferred_element_type=F32)
        u = jnp.dot(nb, wu_ref[:, sl], preferred_element_type=F32)
        a = (g * jax.nn.sigmoid(g) * u).astype(BF16)
        d = jnp.dot(a, wd_ref[sl, :], preferred_element_type=F32)
        acc = d if acc is None else acc + d
    return acc


def _ffn1_inproj_kernel(x_ref, n1_ref, wg_ref, wu_ref, wd_ref, n2_ref, win_ref, h_ref, z_ref):
    x = x_ref[...]
    h = x + 0.5 * _swiglu(_rmsnorm(x, n1_ref[...]).astype(BF16), wg_ref, wu_ref, wd_ref)
    h_ref[...] = h
    n = _rmsnorm(h, n2_ref[...]).astype(BF16)
    z_ref[...] = jnp.dot(n, win_ref[...], preferred_element_type=F32)


def _outproj_ffn2_ple_kernel(mix_ref, h1_ref, p_ref, wo_ref, n1_ref, wg_ref, wu_ref, wd_ref,
                             gn_ref, wpg_ref, wpp_ref, pn_ref, out_ref):
    h2 = h1_ref[...] + jnp.dot(mix_ref[...], wo_ref[...], preferred_element_type=F32)
    h3 = h2 + 0.5 * _swiglu(_rmsnorm(h2, n1_ref[...]).astype(BF16), wg_ref, wu_ref, wd_ref)
    gate = jax.nn.sigmoid(jnp.dot(_rmsnorm(h3, gn_ref[...]).astype(BF16), wpg_ref[...],
                                  preferred_element_type=F32))
    emb = _rmsnorm(jnp.dot(p_ref[...].astype(BF16), wpp_ref[...], preferred_element_type=F32), pn_ref[...])
    out_ref[...] = h3 + gate * emb


def _dot_nt(a, b):
    return lax.dot_general(a, b, (((1,), (1,)), ((), ())), preferred_element_type=F32)


def _dot_tn(a, b):
    return lax.dot_general(a, b, (((0,), (0,)), ((), ())), preferred_element_type=F32)


def _mixer_kernel(sinks_ref, z_ref, cos_ref, sin_ref, qn_ref, kn_ref, wup_ref, bg_ref, gon_ref,
                  mstack_ref, lid_ref, mix_ref, kprev_ref, vprev_ref, state_ref):
    s = pl.program_id(1)

    @pl.when(s == 0)
    def _():
        kprev_ref[...] = jnp.zeros_like(kprev_ref)
        vprev_ref[...] = jnp.zeros_like(vprev_ref)
        state_ref[...] = jnp.zeros_like(state_ref)

    for b in range(SEQS_PER_STEP):
        _mixer_tile(s, sinks_ref, z_ref.at[b], cos_ref, sin_ref, qn_ref, kn_ref, wup_ref, bg_ref, gon_ref,
                    mstack_ref, lid_ref, mix_ref.at[b], kprev_ref.at[b], vprev_ref.at[b], state_ref.at[b])


def _mixer_tile(s, sinks_ref, z_ref, cos_ref, sin_ref, qn_ref, kn_ref, wup_ref, bg_ref, gon_ref,
                mstack_ref, lid_ref, mix_ref, kprev_ref, vprev_ref, state_ref):
    lane = lax.broadcasted_iota(jnp.int32, (TILE, LANES), 1)
    row = lax.broadcasted_iota(jnp.int32, (TILE, LANES), 0)
    slot0 = lane < ATTN_HEAD_DIM
    first_half = (lane % ATTN_HEAD_DIM) < (ATTN_HEAD_DIM // 2)
    upper = lane > row
    cos = cos_ref[...]
    sin = sin_ref[...]

    def qk_norm_rope(xp, gain):
        sq = xp * xp
        s0 = jnp.sum(jnp.where(slot0, sq, 0.0), axis=-1, keepdims=True)
        s1 = jnp.sum(jnp.where(slot0, 0.0, sq), axis=-1, keepdims=True)
        ms = jnp.where(slot0, s0, s1) * (1.0 / ATTN_HEAD_DIM)
        y = xp * lax.rsqrt(ms + EPS) * gain
        partner = jnp.where(first_half, pltpu.roll(y, LANES - ATTN_HEAD_DIM // 2, 1),
                            pltpu.roll(y, ATTN_HEAD_DIM // 2, 1))
        return y * cos + partner * sin

    kb = qk_norm_rope(z_ref[:, Z_AK:Z_AK + LANES], kn_ref[...]).astype(BF16)
    vb = z_ref[:, Z_AV:Z_AV + LANES].astype(BF16)
    k_band = jnp.concatenate([kprev_ref[...], kb], axis=0)
    v_band = jnp.concatenate([vprev_ref[...], vb], axis=0)
    kprev_ref[...] = kb
    vprev_ref[...] = vb
    upper2 = jnp.concatenate([upper, upper], axis=0)
    row2 = lax.broadcasted_iota(jnp.int32, (2 * TILE, 1), 0)
    prev_ok = s > 0
    for p in range(ATTN_HEADS // 2):
        qp = qk_norm_rope(z_ref[:, Z_AQ + p * LANES:Z_AQ + (p + 1) * LANES], qn_ref[...]) * (ATTN_HEAD_DIM ** -0.5)
        q2 = jnp.concatenate([jnp.where(slot0, qp, 0.0), jnp.where(slot0, 0.0, qp)], axis=0).astype(BF16)
        sc = _dot_nt(q2, k_band)
        s_prev = jnp.where(prev_ok, sc[:, :TILE], -jnp.inf)
        sc = jnp.where(upper2, s_prev, sc[:, TILE:])
        sink = jnp.where(row2 < TILE, sinks_ref[p], sinks_ref[p + ATTN_HEADS // 2])
        m = jnp.maximum(jnp.max(sc, axis=-1, keepdims=True), sink)
        e = jnp.exp(sc - m)
        denom = jnp.sum(e, axis=-1, keepdims=True) + jnp.exp(sink - m)
        probs = e * (1.0 / denom)
        pb = jnp.concatenate([jnp.where(upper2, probs, 0.0), jnp.where(upper2, 0.0, probs)], axis=1).astype(BF16)
        o2 = jnp.dot(pb, v_band, preferred_element_type=F32)
        mix_ref[:, p * LANES:(p + 1) * LANES] = jnp.where(slot0, o2[:TILE], o2[TILE:]).astype(BF16)

    x = jnp.dot(z_ref[:, Z_GLR:Z_GLR + LANES].astype(BF16), wup_ref[...], preferred_element_type=F32) + bg_ref[...]
    g = (jnp.minimum(x, 0.0) - jnp.log1p(jnp.exp(-jnp.abs(x)))) * (1.0 / GLA_GATE_TAU)
    g_hi = g.astype(BF16)
    g_lo = (g - g_hi.astype(F32)).astype(BF16)
    g_split = jnp.concatenate([g_hi, g_lo], axis=0)
    decay = jnp.exp(jnp.dot(mstack_ref[...], g_split, preferred_element_type=F32))
    ones = jnp.ones((2 * TILE, LANES), BF16)
    tile_decay = jnp.exp(_dot_tn(g_split, ones))
    lid2 = lid_ref[...]
    for pp in range(GLA_HEADS // 2):
        sl = slice(pp * LANES, (pp + 1) * LANES)
        qp = z_ref[:, Z_GQ + pp * LANES:Z_GQ + (pp + 1) * LANES] * (GLA_DK ** -0.5)
        kp = z_ref[:, Z_GK + pp * LANES:Z_GK + (pp + 1) * LANES]
        q2 = jnp.concatenate([jnp.where(slot0, qp, 0.0), jnp.where(slot0, 0.0, qp)], axis=0)
        attn = _dot_nt(q2.astype(BF16), kp.astype(BF16))
        attn = jnp.where(lid2 == DIAG_LEVEL, attn, 0.0)
        for l in range(N_LEVELS):
            xl = decay[l * TILE:(l + 1) * TILE, sl]
            ql = (q2 * jnp.concatenate([xl, xl], axis=0)).astype(BF16)
            kl = (kp * xl).astype(BF16)
            attn = jnp.where(lid2 == l, _dot_nt(ql, kl), attn)
        xb = decay[N_LEVELS * TILE:(N_LEVELS + 1) * TILE, sl]
        xs = decay[(N_LEVELS + 1) * TILE:(N_LEVELS + 2) * TILE, sl]
        qe = (q2 * jnp.concatenate([xb, xb], axis=0)).astype(BF16)
        st = state_ref[pp * LANES:(pp + 1) * LANES, :]
        stb = st.astype(BF16)
        attn_b = attn.astype(BF16)
        v_wide = z_ref[:, Z_GV + 2 * pp * GLA_DV:Z_GV + 2 * (pp + 1) * GLA_DV].astype(BF16)
        for hh in range(2):
            h = 2 * pp + hh
            lhs = jnp.concatenate([attn_b[hh * TILE:(hh + 1) * TILE], qe[hh * TILE:(hh + 1) * TILE]], axis=1)
            rhs = jnp.concatenate([v_wide[:, hh * GLA_DV:(hh + 1) * GLA_DV], stb], axis=0)
            o = jnp.dot(lhs, rhs, preferred_element_type=F32)
            gog = z_ref[:, Z_GOG + h * GLA_DV:Z_GOG + (h + 1) * GLA_DV]
            o = _rmsnorm(o, gon_ref[...]) * (gog * jax.nn.sigmoid(gog))
            mix_ref[:, ATTN_WIDTH + h * GLA_DV:ATTN_WIDTH + (h + 1) * GLA_DV] = o.astype(BF16)
        upd = _dot_tn((kp * xs).astype(BF16), v_wide)
        td = tile_decay[pp * LANES:(pp + 1) * LANES, :]
        half = GLA_DK
        state_ref[pp * LANES:pp * LANES + half, :] = td[:half] * st[:half] + upd[:half, :GLA_DV]
        state_ref[pp * LANES + half:(pp + 1) * LANES, :] = td[half:] * st[half:] + upd[half:, GLA_DV:]


def _gla_constants():
    r = np.arange(TILE)[:, None]
    t = np.arange(TILE)[None, :]
    mats = []
    for l in range(N_LEVELS):
        size = (TILE // 2) >> l
        same = (r // size) == (t // size)
        odd = ((r // size) % 2) == 1
        mats.append(same & ((odd & (t <= r)) | (~odd & (t > r))))
    mats.append(t <= r)
    mats.append(t > r)
    mstack = np.concatenate(mats, axis=0).astype(np.float32)
    mstack = np.concatenate([mstack, mstack], axis=1)
    x = r ^ t
    top_bit = np.floor(np.log2(np.maximum(x, 1))).astype(np.int32)
    lid = np.where(t < r, (N_LEVELS - 1) - top_bit, np.where(t == r, DIAG_LEVEL, -1)).astype(np.int32)
    return jnp.asarray(mstack, BF16), jnp.asarray(np.concatenate([lid, lid], axis=0))


def _rope_tables(seq):
    dh = ATTN_HEAD_DIM
    inv_freq = ROPE_THETA ** (-jnp.arange(0, dh, 2, dtype=F32) / dh)
    ang = jnp.arange(seq).astype(F32)[:, None] * inv_freq[None, :]
    cos, sin = jnp.cos(ang), jnp.sin(ang)
    cos_t = jnp.tile(jnp.concatenate([cos, cos], axis=-1), (1, LANES // dh))
    sin_t = jnp.tile(jnp.concatenate([-sin, sin], axis=-1), (1, LANES // dh))
    return cos_t, sin_t


def _pair_perm():
    heads = np.stack([np.arange(4), np.arange(4) + 4], axis=1).reshape(-1)
    return (heads[:, None] * ATTN_HEAD_DIM + np.arange(ATTN_HEAD_DIM)[None, :]).reshape(-1)


def _prep_w_in(w_in):
    aq, ak, av, gq, gk, gv, glr, gog = jnp.split(
        w_in, np.cumsum([ATTN_WIDTH, ATTN_KV_WIDTH, ATTN_KV_WIDTH, GLA_QK_WIDTH, GLA_QK_WIDTH, GLA_WIDTH,
                         GLA_GATE_RANK]).tolist(), axis=-1)
    glr = jnp.pad(glr, ((0, 0), (0, LANES - GLA_GATE_RANK)))
    return jnp.concatenate([aq[:, _pair_perm()], ak, av, gq, gk, gv, gog, glr], axis=-1).astype(BF16)


def _resident(shape):
    return pl.BlockSpec(shape, lambda *_: (0,) * len(shape), pipeline_mode=pl.Buffered(1))


def _layer(h, p, ffn1_norm, ffn1_w_gate, ffn1_w_up, ffn1_w_down, mix_norm, w_in, q_norm, k_norm, attn_sinks,
           gla_w_gate_up, gla_b_gate, gla_out_norm, w_out, ffn2_norm, ffn2_w_gate, ffn2_w_up, ffn2_w_down,
           ple_gate_norm, ple_w_gate, ple_w_proj, ple_norm):
    batch, seq, d = h.shape
    tokens = batch * seq
    x2 = h.reshape(tokens, d)
    rows = lambda width: pl.BlockSpec((ROW_TILE, width), lambda i: (i, 0))
    vec = lambda v: v.reshape(1, -1).astype(F32)

    h1, z = pl.pallas_call(
        _ffn1_inproj_kernel,
        grid=(tokens // ROW_TILE,),
        in_specs=[rows(d), _resident((1, d)), _resident((d, D_FF)), _resident((d, D_FF)), _resident((D_FF, d)),
                  _resident((1, d)), _resident((d, Z_WIDTH))],
        out_specs=[rows(d), rows(Z_WIDTH)],
        out_shape=[jax.ShapeDtypeStruct((tokens, d), F32), jax.ShapeDtypeStruct((tokens, Z_WIDTH), F32)],
        compiler_params=pltpu.CompilerParams(dimension_semantics=("arbitrary",), vmem_limit_bytes=VMEM_LIMIT),
        name="ffn1_inproj",
    )(x2, vec(ffn1_norm), ffn1_w_gate.astype(BF16), ffn1_w_up.astype(BF16), ffn1_w_down.astype(BF16),
      vec(mix_norm), _prep_w_in(w_in))

    cos_t, sin_t = _rope_tables(seq)
    mstack, lid2 = _gla_constants()
    tiles = seq // TILE
    wup = jnp.pad(gla_w_gate_up, ((0, LANES - GLA_GATE_RANK), (0, 0))).astype(BF16)
    full = lambda shape: pl.BlockSpec(shape, lambda b, s: (0,) * len(shape))
    nseq = SEQS_PER_STEP
    mix = pl.pallas_call(
        _mixer_kernel,
        grid=(batch // nseq, tiles),
        in_specs=[pl.BlockSpec(memory_space=pltpu.SMEM),
                  pl.BlockSpec((nseq, TILE, Z_WIDTH), lambda b, s: (b, s, 0)),
                  pl.BlockSpec((TILE, LANES), lambda b, s: (s, 0)),
                  pl.BlockSpec((TILE, LANES), lambda b, s: (s, 0)),
                  full((1, LANES)), full((1, LANES)), full((LANES, GLA_QK_WIDTH)), full((1, GLA_QK_WIDTH)),
                  full((1, GLA_DV)), full(((N_LEVELS + 2) * TILE, 2 * TILE)), full((2 * TILE, TILE))],
        out_specs=pl.BlockSpec((nseq, TILE, d), lambda b, s: (b, s, 0)),
        out_shape=jax.ShapeDtypeStruct((batch, seq, d), BF16),
        scratch_shapes=[pltpu.VMEM((nseq, TILE, LANES), BF16), pltpu.VMEM((nseq, TILE, LANES), BF16),
                        pltpu.VMEM((nseq, GLA_QK_WIDTH, GLA_DV), F32)],
        compiler_params=pltpu.CompilerParams(dimension_semantics=("arbitrary", "arbitrary")),
        name="mixer",
    )(attn_sinks.astype(F32), z.reshape(batch, seq, Z_WIDTH), cos_t, sin_t, vec(jnp.tile(q_norm, 2)),
      vec(jnp.tile(k_norm, 2)), wup, vec(gla_b_gate), vec(gla_out_norm), mstack, lid2)
    mix = mix.reshape(tokens, d)

    wo = jnp.concatenate([w_out[:ATTN_WIDTH][_pair_perm()], w_out[ATTN_WIDTH:]], axis=0).astype(BF16)
    out = pl.pallas_call(
        _outproj_ffn2_ple_kernel,
        grid=(tokens // ROW_TILE,),
        in_specs=[rows(d), rows(d), rows(PLE_DIM), _resident((d, d)), _resident((1, d)), _resident((d, D_FF)),
                  _resident((d, D_FF)), _resident((D_FF, d)), _resident((1, d)), _resident((d, d)),
                  _resident((PLE_DIM, d)), _resident((1, d))],
        out_specs=rows(d),
        out_shape=jax.ShapeDtypeStruct((tokens, d), F32),
        compiler_params=pltpu.CompilerParams(dimension_semantics=("arbitrary",), vmem_limit_bytes=VMEM_LIMIT),
        name="outproj_ffn2_ple",
    )(mix, h1, p.reshape(tokens, PLE_DIM), wo, vec(ffn2_norm), ffn2_w_gate.astype(BF16), ffn2_w_up.astype(BF16),
      ffn2_w_down.astype(BF16), vec(ple_gate_norm), ple_w_gate.astype(BF16), ple_w_proj.astype(BF16),
      vec(ple_norm))
    return out.reshape(batch, seq, d)


@jax.jit
def kernel(x, p, ffn1_norm, ffn1_w_gate, ffn1_w_up, ffn1_w_down, mix_norm, w_in, q_norm, k_norm, attn_sinks,
           gla_w_gate_up, gla_b_gate, gla_out_norm, w_out, ffn2_norm, ffn2_w_gate, ffn2_w_up, ffn2_w_down,
           ple_gate_norm, ple_w_gate, ple_w_proj, ple_norm):
    h = x
    for i in range(p.shape[0]):
        h = _layer(h, p[i], ffn1_norm[i], ffn1_w_gate[i], ffn1_w_up[i], ffn1_w_down[i], mix_norm[i], w_in[i],
                   q_norm[i], k_norm[i], attn_sinks[i], gla_w_gate_up[i], gla_b_gate[i], gla_out_norm[i], w_out[i],
                   ffn2_norm[i], ffn2_w_gate[i], ffn2_w_up[i], ffn2_w_down[i], ple_gate_norm[i], ple_w_gate[i],
                   ple_w_proj[i], ple_norm[i])
    return h
```

```python
import functools

import numpy as np
import jax
import jax.numpy as jnp
from jax import lax
from jax.experimental import pallas as pl
from jax.experimental.pallas import tpu as pltpu

F32 = jnp.float32
BF16 = jnp.bfloat16

D_MODEL = 1024
D_FF = 2816
PLE_DIM = 256
EPS = 1e-6

ATTN_HEAD_DIM = 64
ATTN_HEADS = 8
ATTN_KV_HEADS = 2
ATTN_WIDTH = ATTN_HEADS * ATTN_HEAD_DIM
ATTN_KV_WIDTH = ATTN_KV_HEADS * ATTN_HEAD_DIM
ROPE_THETA = 10000.0

GLA_HEADS = 4
GLA_DV = 128
GLA_DK = 64
GLA_QK_WIDTH = GLA_HEADS * GLA_DK
GLA_WIDTH = GLA_HEADS * GLA_DV
GLA_GATE_RANK = 16
GLA_GATE_TAU = 16.0

LANES = 128
TILE = 128
SEQS_PER_STEP = 2
ROW_TILE = 512
MXU_DIM = 256
FF_CHUNKS = ((0, 6 * MXU_DIM), (6 * MXU_DIM, D_FF))
VMEM_LIMIT = 56 * 1024 * 1024

Z_AQ, Z_AK, Z_AV = 0, 512, 640
Z_GQ, Z_GK, Z_GV, Z_GOG, Z_GLR = 768, 1024, 1280, 1792, 2304
Z_WIDTH = Z_GLR + LANES

N_LEVELS = 7
DIAG_LEVEL = N_LEVELS


def _rmsnorm(x, gain):
    return x * lax.rsqrt(jnp.mean(x * x, axis=-1, keepdims=True) + EPS) * gain


def _swiglu(nb, wg_ref, wu_ref, wd_ref):
    acc = None
    for lo, hi in FF_CHUNKS:
        sl = slice(lo, hi)
        g = jnp.dot(nb, wg_ref[:, sl], preferred_element_type=F32)
        u = jnp.dot(nb, wu_ref[:, sl], preferred_element_type=F32)
        a = (g * jax.nn.sigmoid(g) * u).astype(BF16)
        d = jnp.dot(a, wd_ref[sl, :], preferred_element_type=F32)
        acc = d if acc is None else acc + d
    return acc


def _ffn1_inproj_kernel(x_ref, n1_ref, wg_ref, wu_ref, wd_ref, n2_ref, win_ref, h_ref, z_ref):
    x = x_ref[...]
    h = x + 0.5 * _swiglu(_rmsnorm(x, n1_ref[...]).astype(BF16), wg_ref, wu_ref, wd_ref)
    h_ref[...] = h
    n = _rmsnorm(h, n2_ref[...]).astype(BF16)
    z_ref[...] = jnp.dot(n, win_ref[...], preferred_element_type=F32)


def _outproj_ffn2_ple_kernel(mix_ref, h1_ref, p_ref, wo_ref, n1_ref, wg_ref, wu_ref, wd_ref,
                             gn_ref, wpg_ref, wpp_ref, pn_ref, out_ref):
    h2 = h1_ref[...] + jnp.dot(mix_ref[...], wo_ref[...], preferred_element_type=F32)
    h3 = h2 + 0.5 * _swiglu(_rmsnorm(h2, n1_ref[...]).astype(BF16), wg_ref, wu_ref, wd_ref)
    gate = jax.nn.sigmoid(jnp.dot(_rmsnorm(h3, gn_ref[...]).astype(BF16), wpg_ref[...],
                                  preferred_element_type=F32))
    emb = _rmsnorm(jnp.dot(p_ref[...].astype(BF16), wpp_ref[...], preferred_element_type=F32), pn_ref[...])
    out_ref[...] = h3 + gate * emb


def _dot_nt(a, b):
    return lax.dot_general(a, b, (((1,), (1,)), ((), ())), preferred_element_type=F32)


def _dot_tn(a, b):
    return lax.dot_general(a, b, (((0,), (0,)), ((), ())), preferred_element_type=F32)


def _mixer_kernel(sinks_ref, z_ref, cos_ref, sin_ref, qn_ref, kn_ref, wup_ref, bg_ref, gon_ref,
                  mstack_ref, lid_ref, mix_ref, kprev_ref, vprev_ref, state_ref):
    s = pl.program_id(1)

    @pl.when(s == 0)
    def _():
        kprev_ref[...] = jnp.zeros_like(kprev_ref)
        vprev_ref[...] = jnp.zeros_like(vprev_ref)
        state_ref[...] = jnp.zeros_like(state_ref)

    for b in range(SEQS_PER_STEP):
        _mixer_tile(s, sinks_ref, z_ref.at[b], cos_ref, sin_ref, qn_ref, kn_ref, wup_ref, bg_ref, gon_ref,
                    mstack_ref, lid_ref, mix_ref.at[b], kprev_ref.at[b], vprev_ref.at[b], state_ref.at[b])


def _mixer_tile(s, sinks_ref, z_ref, cos_ref, sin_ref, qn_ref, kn_ref, wup_ref, bg_ref, gon_ref,
                mstack_ref, lid_ref, mix_ref, kprev_ref, vprev_ref, state_ref):
    lane = lax.broadcasted_iota(jnp.int32, (TILE, LANES), 1)
    row = lax.broadcasted_iota(jnp.int32, (TILE, LANES), 0)
    slot0 = lane < ATTN_HEAD_DIM
    first_half = (lane % ATTN_HEAD_DIM) < (ATTN_HEAD_DIM // 2)
    upper = lane > row
    cos = cos_ref[...]
    sin = sin_ref[...]

    def qk_norm_rope(xp, gain):
        sq = xp * xp
        s0 = jnp.sum(jnp.where(slot0, sq, 0.0), axis=-1, keepdims=True)
        s1 = jnp.sum(jnp.where(slot0, 0.0, sq), axis=-1, keepdims=True)
        ms = jnp.where(slot0, s0, s1) * (1.0 / ATTN_HEAD_DIM)
        y = xp * lax.rsqrt(ms + EPS) * gain
        partner = jnp.where(first_half, pltpu.roll(y, LANES - ATTN_HEAD_DIM // 2, 1),
                            pltpu.roll(y, ATTN_HEAD_DIM // 2, 1))
        return y * cos + partner * sin

    kb = qk_norm_rope(z_ref[:, Z_AK:Z_AK + LANES], kn_ref[...]).astype(BF16)
    vb = z_ref[:, Z_AV:Z_AV + LANES].astype(BF16)
    k_band = jnp.concatenate([kprev_ref[...], kb], axis=0)
    v_band = jnp.concatenate([vprev_ref[...], vb], axis=0)
    kprev_ref[...] = kb
    vprev_ref[...] = vb
    upper2 = jnp.concatenate([upper, upper], axis=0)
    row2 = lax.broadcasted_iota(jnp.int32, (2 * TILE, 1), 0)
    prev_ok = s > 0
    for p in range(ATTN_HEADS // 2):
        qp = qk_norm_rope(z_ref[:, Z_AQ + p * LANES:Z_AQ + (p + 1) * LANES], qn_ref[...]) * (ATTN_HEAD_DIM ** -0.5)
        q2 = jnp.concatenate([jnp.where(slot0, qp, 0.0), jnp.where(slot0, 0.0, qp)], axis=0).astype(BF16)
        sc = _dot_nt(q2, k_band)
        s_prev = jnp.where(prev_ok, sc[:, :TILE], -jnp.inf)
        sc = jnp.where(upper2, s_prev, sc[:, TILE:])
        sink = jnp.where(row2 < TILE, sinks_ref[p], sinks_ref[p + ATTN_HEADS // 2])
        m = jnp.maximum(jnp.max(sc, axis=-1, keepdims=True), sink)
        e = jnp.exp(sc - m)
        denom = jnp.sum(e, axis=-1, keepdims=True) + jnp.exp(sink - m)
        probs = e * (1.0 / denom)
        pb = jnp.concatenate([jnp.where(upper2, probs, 0.0), jnp.where(upper2, 0.0, probs)], axis=1).astype(BF16)
        o2 = jnp.dot(pb, v_band, preferred_element_type=F32)
        mix_ref[:, p * LANES:(p + 1) * LANES] = jnp.where(slot0, o2[:TILE], o2[TILE:]).astype(BF16)

    x = jnp.dot(z_ref[:, Z_GLR:Z_GLR + LANES].astype(BF16), wup_ref[...], preferred_element_type=F32) + bg_ref[...]
    g = (jnp.minimum(x, 0.0) - jnp.log1p(jnp.exp(-jnp.abs(x)))) * (1.0 / GLA_GATE_TAU)
    g_hi = g.astype(BF16)
    g_lo = (g - g_hi.astype(F32)).astype(BF16)
    g_split = jnp.concatenate([g_hi, g_lo], axis=0)
    decay = jnp.exp(jnp.dot(mstack_ref[...], g_split, preferred_element_type=F32))
    ones = jnp.ones((2 * TILE, LANES), BF16)
    tile_decay = jnp.exp(_dot_tn(g_split, ones))
    lid2 = lid_ref[...]
    for pp in range(GLA_HEADS // 2):
        sl = slice(pp * LANES, (pp + 1) * LANES)
        qp = z_ref[:, Z_GQ + pp * LANES:Z_GQ + (pp + 1) * LANES] * (GLA_DK ** -0.5)
        kp = z_ref[:, Z_GK + pp * LANES:Z_GK + (pp + 1) * LANES]
        q2 = jnp.concatenate([jnp.where(slot0, qp, 0.0), jnp.where(slot0, 0.0, qp)], axis=0)
        attn = _dot_nt(q2.astype(BF16), kp.astype(BF16))
        attn = jnp.where(lid2 == DIAG_LEVEL, attn, 0.0)
        for l in range(N_LEVELS):
            xl = decay[l * TILE:(l + 1) * TILE, sl]
            ql = (q2 * jnp.concatenate([xl, xl], axis=0)).astype(BF16)
            kl = (kp * xl).astype(BF16)
            attn = jnp.where(lid2 == l, _dot_nt(ql, kl), attn)
        xb = decay[N_LEVELS * TILE:(N_LEVELS + 1) * TILE, sl]
        xs = decay[(N_LEVELS + 1) * TILE:(N_LEVELS + 2) * TILE, sl]
        qe = (q2 * jnp.concatenate([xb, xb], axis=0)).astype(BF16)
        st = state_ref[pp * LANES:(pp + 1) * LANES, :]
        stb = st.astype(BF16)
        attn_b = attn.astype(BF16)
        v_wide = z_ref[:, Z_GV + 2 * pp * GLA_DV:Z_GV + 2 * (pp + 1) * GLA_DV].astype(BF16)
        for hh in range(2):
            h = 2 * pp + hh
            lhs = jnp.concatenate([attn_b[hh * TILE:(hh + 1) * TILE], qe[hh * TILE:(hh + 1) * TILE]], axis=1)
            rhs = jnp.concatenate([v_wide[:, hh * GLA_DV:(hh + 1) * GLA_DV], stb], axis=0)
            o = jnp.dot(lhs, rhs, preferred_element_type=F32)
            gog = z_ref[:, Z_GOG + h * GLA_DV:Z_GOG + (h + 1) * GLA_DV]
            o = _rmsnorm(o, gon_ref[...]) * (gog * jax.nn.sigmoid(gog))
            mix_ref[:, ATTN_WIDTH + h * GLA_DV:ATTN_WIDTH + (h + 1) * GLA_DV] = o.astype(BF16)
        upd = _dot_tn((kp * xs).astype(BF16), v_wide)
        td = tile_decay[pp * LANES:(pp + 1) * LANES, :]
        half = GLA_DK
        state_ref[pp * LANES:pp * LANES + half, :] = td[:half] * st[:half] + upd[:half, :GLA_DV]
        state_ref[pp * LANES + half:(pp + 1) * LANES, :] = td[half:] * st[half:] + upd[half:, GLA_DV:]


def _gla_constants():
    r = np.arange(TILE)[:, None]
    t = np.arange(TILE)[None, :]
    mats = []
    for l in range(N_LEVELS):
        size = (TILE // 2) >> l
        same = (r // size) == (t // size)
        odd = ((r // size) % 2) == 1
        mats.append(same & ((odd & (t <= r)) | (~odd & (t > r))))
    mats.append(t <= r)
    mats.append(t > r)
    mstack = np.concatenate(mats, axis=0).astype(np.float32)
    mstack = np.concatenate([mstack, mstack], axis=1)
    x = r ^ t
    top_bit = np.floor(np.log2(np.maximum(x, 1))).astype(np.int32)
    lid = np.where(t < r, (N_LEVELS - 1) - top_bit, np.where(t == r, DIAG_LEVEL, -1)).astype(np.int32)
    return jnp.asarray(mstack, BF16), jnp.asarray(np.concatenate([lid, lid], axis=0))


def _rope_tables(seq):
    dh = ATTN_HEAD_DIM
    inv_freq = ROPE_THETA ** (-jnp.arange(0, dh, 2, dtype=F32) / dh)
    ang = jnp.arange(seq).astype(F32)[:, None] * inv_freq[None, :]
    cos, sin = jnp.cos(ang), jnp.sin(ang)
    cos_t = jnp.tile(jnp.concatenate([cos, cos], axis=-1), (1, LANES // dh))
    sin_t = jnp.tile(jnp.concatenate([-sin, sin], axis=-1), (1, LANES // dh))
    return cos_t, sin_t


def _pair_perm():
    heads = np.stack([np.arange(4), np.arange(4) + 4], axis=1).reshape(-1)
    return (heads[:, None] * ATTN_HEAD_DIM + np.arange(ATTN_HEAD_DIM)[None, :]).reshape(-1)


def _prep_w_in(w_in):
    aq, ak, av, gq, gk, gv, glr, gog = jnp.split(
        w_in, np.cumsum([ATTN_WIDTH, ATTN_KV_WIDTH, ATTN_KV_WIDTH, GLA_QK_WIDTH, GLA_QK_WIDTH, GLA_WIDTH,
                         GLA_GATE_RANK]).tolist(), axis=-1)
    glr = jnp.pad(glr, ((0, 0), (0, LANES - GLA_GATE_RANK)))
    return jnp.concatenate([aq[:, _pair_perm()], ak, av, gq, gk, gv, gog, glr], axis=-1).astype(BF16)


def _resident(shape):
    return pl.BlockSpec(shape, lambda *_: (0,) * len(shape), pipeline_mode=pl.Buffered(1))


def _layer(h, p, ffn1_norm, ffn1_w_gate, ffn1_w_up, ffn1_w_down, mix_norm, w_in, q_norm, k_norm, attn_sinks,
           gla_w_gate_up, gla_b_gate, gla_out_norm, w_out, ffn2_norm, ffn2_w_gate, ffn2_w_up, ffn2_w_down,
           ple_gate_norm, ple_w_gate, ple_w_proj, ple_norm):
    batch, seq, d = h.shape
    tokens = batch * seq
    x2 = h.reshape(tokens, d)
    rows = lambda width: pl.BlockSpec((ROW_TILE, width), lambda i: (i, 0))
    vec = lambda v: v.reshape(1, -1).astype(F32)

    h1, z = pl.pallas_call(
        _ffn1_inproj_kernel,
        grid=(tokens // ROW_TILE,),
        in_specs=[rows(d), _resident((1, d)), _resident((d, D_FF)), _resident((d, D_FF)), _resident((D_FF, d)),
                  _resident((1, d)), _resident((d, Z_WIDTH))],
        out_specs=[rows(d), rows(Z_WIDTH)],
        out_shape=[jax.ShapeDtypeStruct((tokens, d), F32), jax.ShapeDtypeStruct((tokens, Z_WIDTH), F32)],
        compiler_params=pltpu.CompilerParams(dimension_semantics=("arbitrary",), vmem_limit_bytes=VMEM_LIMIT),
        name="ffn1_inproj",
    )(x2, vec(ffn1_norm), ffn1_w_gate.astype(BF16), ffn1_w_up.astype(BF16), ffn1_w_down.astype(BF16),
      vec(mix_norm), _prep_w_in(w_in))

    cos_t, sin_t = _rope_tables(seq)
    mstack, lid2 = _gla_constants()
    tiles = seq // TILE
    wup = jnp.pad(gla_w_gate_up, ((0, LANES - GLA_GATE_RANK), (0, 0))).astype(BF16)
    full = lambda shape: pl.BlockSpec(shape, lambda b, s: (0,) * len(shape))
    nseq = SEQS_PER_STEP
    mix = pl.pallas_call(
        _mixer_kernel,
        grid=(batch // nseq, tiles),
        in_specs=[pl.BlockSpec(memory_space=pltpu.SMEM),
                  pl.BlockSpec((nseq, TILE, Z_WIDTH), lambda b, s: (b, s, 0)),
                  pl.BlockSpec((TILE, LANES), lambda b, s: (s, 0)),
                  pl.BlockSpec((TILE, LANES), lambda b, s: (s, 0)),
                  full((1, LANES)), full((1, LANES)), full((LANES, GLA_QK_WIDTH)), full((1, GLA_QK_WIDTH)),
                  full((1, GLA_DV)), full(((N_LEVELS + 2) * TILE, 2 * TILE)), full((2 * TILE, TILE))],
        out_specs=pl.BlockSpec((nseq, TILE, d), lambda b, s: (b, s, 0)),
        out_shape=jax.ShapeDtypeStruct((batch, seq, d), BF16),
        scratch_shapes=[pltpu.VMEM((nseq, TILE, LANES), BF16), pltpu.VMEM((nseq, TILE, LANES), BF16),
                        pltpu.VMEM((nseq, GLA_QK_WIDTH, GLA_DV), F32)],
        compiler_params=pltpu.CompilerParams(dimension_semantics=("arbitrary", "arbitrary")),
        name="mixer",
    )(attn_sinks.astype(F32), z.reshape(batch, seq, Z_WIDTH), cos_t, sin_t, vec(jnp.tile(q_norm, 2)),
      vec(jnp.tile(k_norm, 2)), wup, vec(gla_b_gate), vec(gla_out_norm), mstack, lid2)
    mix = mix.reshape(tokens, d)

    wo = jnp.concatenate([w_out[:ATTN_WIDTH][_pair_perm()], w_out[ATTN_WIDTH:]], axis=0).astype(BF16)
    out = pl.pallas_call(
        _outproj_ffn2_ple_kernel,
        grid=(tokens // ROW_TILE,),
        in_specs=[rows(d), rows(d), rows(PLE_DIM), _resident((d, d)), _resident((1, d)), _resident((d, D_FF)),
                  _resident((d, D_FF)), _resident((D_FF, d)), _resident((1, d)), _resident((d, d)),
                  _resident((PLE_DIM, d)), _resident((1, d))],
        out_specs=rows(d),
        out_shape=jax.ShapeDtypeStruct((tokens, d), F32),
        compiler_params=pltpu.CompilerParams(dimension_semantics=("arbitrary",), vmem_limit_bytes=VMEM_LIMIT),
        name="outproj_ffn2_ple",
    )(mix, h1, p.reshape(tokens, PLE_DIM), wo, vec(ffn2_norm), ffn2_w_gate.astype(BF16), ffn2_w_up.astype(BF16),
      ffn2_w_down.astype(BF16), vec(ple_gate_norm), ple_w_gate.astype(BF16), ple_w_proj.astype(BF16),
      vec(ple_norm))
    return out.reshape(batch, seq, d)


@jax.jit
def kernel(x, p, ffn1_norm, ffn1_w_gate, ffn1_w_up, ffn1_w_down, mix_norm, w_in, q_norm, k_norm, attn_sinks,
           gla_w_gate_up, gla_b_gate, gla_out_norm, w_out, ffn2_norm, ffn2_w_gate, ffn2_w_up, ffn2_w_down,
           ple_gate_norm, ple_w_gate, ple_w_proj, ple_norm):
    h = x
    for i in range(p.shape[0]):
        h = _layer(h, p[i], ffn1_norm[i], ffn1_w_gate[i], ffn1_w_up[i], ffn1_w_down[i], mix_norm[i], w_in[i],
                   q_norm[i], k_norm[i], attn_sinks[i], gla_w_gate_up[i], gla_b_gate[i], gla_out_norm[i], w_out[i],
                   ffn2_norm[i], ffn2_w_gate[i], ffn2_w_up[i], ffn2_w_down[i], ple_gate_norm[i], ple_w_gate[i],
                   ple_w_proj[i], ple_norm[i])
    return h
```

```python
import itertools

import numpy as np
import jax
import jax.numpy as jnp
from jax import lax
from jax.experimental import pallas as pl
from jax.experimental.pallas import tpu as pltpu

F32 = jnp.float32
BF16 = jnp.bfloat16

D_MODEL = 1024
D_FF = 2816
PLE_DIM = 256
EPS = 1e-6

ATTN_HEAD_DIM = 64
ATTN_HEADS = 8
ATTN_KV_HEADS = 2
ATTN_WIDTH = ATTN_HEADS * ATTN_HEAD_DIM
ATTN_KV_WIDTH = ATTN_KV_HEADS * ATTN_HEAD_DIM
ROPE_THETA = 10000.0

GLA_HEADS = 4
GLA_DV = 128
GLA_DK = 64
GLA_QK_WIDTH = GLA_HEADS * GLA_DK
GLA_WIDTH = GLA_HEADS * GLA_DV
GLA_GATE_RANK = 16
GLA_GATE_TAU = 16.0

LANES = 128
TILE = 128
ROW_TILE = 512
BLOCKS_PER_SEQ = 8
MXU_DIM = 256
FF_CHUNKS = ((0, 6 * MXU_DIM), (6 * MXU_DIM, D_FF))
MIXER_STAGES = 14
VMEM_LIMIT = 56 * 1024 * 1024
FUSED_FLAGS = {}

Z_AQ, Z_AK, Z_AV = 0, 512, 640
Z_GQ, Z_GK, Z_GV, Z_GOG, Z_GLR = 768, 1024, 1280, 1792, 2304
Z_WIDTH = Z_GLR + LANES

N_LEVELS = 7
DIAG_LEVEL = N_LEVELS


def _rmsnorm(x, gain):
    return x * lax.rsqrt(jnp.mean(x * x, axis=-1, keepdims=True) + EPS) * gain


def _swiglu(nb, wg_ref, wu_ref, wd_ref, chunks=FF_CHUNKS, between=()):
    acc = None
    for c, (lo, hi) in enumerate(chunks):
        sl = slice(lo, hi)
        g = jnp.dot(nb, wg_ref[:, sl], preferred_element_type=F32)
        u = jnp.dot(nb, wu_ref[:, sl], preferred_element_type=F32)
        a = (g * jax.nn.sigmoid(g) * u).astype(BF16)
        d = jnp.dot(a, wd_ref[sl, :], preferred_element_type=F32)
        acc = d if acc is None else acc + d
        if c < len(between):
            between[c]()
    return acc


def _swiglu_units(nb, wg_ref, wu_ref, wd_ref, acc_box):
    acc = None
    for lo in range(0, D_FF, MXU_DIM):
        sl = slice(lo, lo + MXU_DIM)
        g = jnp.dot(nb, wg_ref[:, sl], preferred_element_type=F32)
        yield
        u = jnp.dot(nb, wu_ref[:, sl], preferred_element_type=F32)
        yield
        a = (g * jax.nn.sigmoid(g) * u).astype(BF16)
        d = jnp.dot(a, wd_ref[sl, :], preferred_element_type=F32)
        acc = d if acc is None else acc + d
        if lo + MXU_DIM < D_FF:
            yield
    acc_box.append(acc)
    yield


def _ffn1_mixer_kernel(sinks_ref, x_ref, n1_ref, wg_ref, wu_ref, wd_ref, n2_ref, win_ref,
                       cos_ref, sin_ref, qn_ref, kn_ref, wup_ref, bg_ref, gon_ref, mstack_ref, lid_ref,
                       h_ref, mix_ref, n_ref, z_ref, kprev_ref, vprev_ref, state_ref):
    t = pl.program_id(0)
    blk = jnp.maximum(t - 1, 0) % BLOCKS_PER_SEQ

    @pl.when(t == 0)
    def _():
        n_ref[...] = jnp.zeros_like(n_ref)

    @pl.when(blk == 0)
    def _():
        kprev_ref[...] = jnp.zeros_like(kprev_ref)
        vprev_ref[...] = jnp.zeros_like(vprev_ref)
        state_ref[...] = jnp.zeros_like(state_ref)

    z_ref[...] = jnp.dot(n_ref[...], win_ref[...], preferred_element_type=F32)

    def mixer_tile(j):
        rows = pl.ds(j * TILE, TILE)
        prev_ok = (blk > 0) if j == 0 else True
        return _mixer_tile(prev_ok, sinks_ref, z_ref.at[rows], cos_ref.at[rows], sin_ref.at[rows], qn_ref, kn_ref,
                           wup_ref, bg_ref, gon_ref, mstack_ref, lid_ref, mix_ref.at[rows], kprev_ref, vprev_ref,
                           state_ref)

    acc_box = []
    ffn = _swiglu_units(_rmsnorm(x_ref[...], n1_ref[...]).astype(BF16), wg_ref, wu_ref, wd_ref, acc_box)
    mixer = itertools.chain.from_iterable(mixer_tile(j) for j in range(ROW_TILE // TILE))
    n_mixer = (ROW_TILE // TILE) * MIXER_STAGES
    n_ffn = 3 * (D_FF // MXU_DIM)
    done = 0
    for i, _ in enumerate(mixer):
        while done * n_mixer < (i + 1) * n_ffn:
            next(ffn)
            done += 1
    assert i + 1 == n_mixer and done == n_ffn and next(ffn, None) is None
    h = x_ref[...] + 0.5 * acc_box[0]
    h_ref[...] = h
    n_ref[...] = _rmsnorm(h, n2_ref[...]).astype(BF16)


def _outproj_ffn2_ple_kernel(mix_ref, h1_ref, p_ref, wo_ref, n1_ref, wg_ref, wu_ref, wd_ref,
                             gn_ref, wpg_ref, wpp_ref, pn_ref, out_ref):
    h2 = h1_ref[...] + jnp.dot(mix_ref[...], wo_ref[...], preferred_element_type=F32)
    h3 = h2 + 0.5 * _swiglu(_rmsnorm(h2, n1_ref[...]).astype(BF16), wg_ref, wu_ref, wd_ref)
    gate = jax.nn.sigmoid(jnp.dot(_rmsnorm(h3, gn_ref[...]).astype(BF16), wpg_ref[...],
                                  preferred_element_type=F32))
    emb = _rmsnorm(jnp.dot(p_ref[...].astype(BF16), wpp_ref[...], preferred_element_type=F32), pn_ref[...])
    out_ref[...] = h3 + gate * emb


def _dot_nt(a, b):
    return lax.dot_general(a, b, (((1,), (1,)), ((), ())), preferred_element_type=F32)


def _dot_tn(a, b):
    return lax.dot_general(a, b, (((0,), (0,)), ((), ())), preferred_element_type=F32)


def _mixer_tile(prev_ok, sinks_ref, z_ref, cos_ref, sin_ref, qn_ref, kn_ref, wup_ref, bg_ref, gon_ref,
                mstack_ref, lid_ref, mix_ref, kprev_ref, vprev_ref, state_ref):
    lane = lax.broadcasted_iota(jnp.int32, (TILE, LANES), 1)
    row = lax.broadcasted_iota(jnp.int32, (TILE, LANES), 0)
    slot0 = lane < ATTN_HEAD_DIM
    first_half = (lane % ATTN_HEAD_DIM) < (ATTN_HEAD_DIM // 2)
    upper = lane > row
    cos = cos_ref[...]
    sin = sin_ref[...]

    def qk_norm_rope(xp, gain):
        sq = xp * xp
        s0 = jnp.sum(jnp.where(slot0, sq, 0.0), axis=-1, keepdims=True)
        s1 = jnp.sum(jnp.where(slot0, 0.0, sq), axis=-1, keepdims=True)
        ms = jnp.where(slot0, s0, s1) * (1.0 / ATTN_HEAD_DIM)
        y = xp * lax.rsqrt(ms + EPS) * gain
        partner = jnp.where(first_half, pltpu.roll(y, LANES - ATTN_HEAD_DIM // 2, 1),
                            pltpu.roll(y, ATTN_HEAD_DIM // 2, 1))
        return y * cos + partner * sin

    kb = qk_norm_rope(z_ref[:, Z_AK:Z_AK + LANES], kn_ref[...]).astype(BF16)
    vb = z_ref[:, Z_AV:Z_AV + LANES].astype(BF16)
    k_band = jnp.concatenate([kprev_ref[...], kb], axis=0)
    v_band = jnp.concatenate([vprev_ref[...], vb], axis=0)
    kprev_ref[...] = kb
    vprev_ref[...] = vb
    upper2 = jnp.concatenate([upper, upper], axis=0)
    row2 = lax.broadcasted_iota(jnp.int32, (2 * TILE, 1), 0)

    def scores(p):
        qp = qk_norm_rope(z_ref[:, Z_AQ + p * LANES:Z_AQ + (p + 1) * LANES], qn_ref[...]) * (ATTN_HEAD_DIM ** -0.5)
        q2 = jnp.concatenate([jnp.where(slot0, qp, 0.0), jnp.where(slot0, 0.0, qp)], axis=0).astype(BF16)
        sc = _dot_nt(q2, k_band)
        s_prev = sc[:, :TILE] if prev_ok is True else jnp.where(prev_ok, sc[:, :TILE], -jnp.inf)
        return jnp.where(upper2, s_prev, sc[:, TILE:])

    def softmax_pv(p, sc):
        sink = jnp.where(row2 < TILE, sinks_ref[p], sinks_ref[p + ATTN_HEADS // 2])
        m = jnp.maximum(jnp.max(sc, axis=-1, keepdims=True), sink)
        e = jnp.exp(sc - m)
        denom = jnp.sum(e, axis=-1, keepdims=True) + jnp.exp(sink - m)
        probs = e * (1.0 / denom)
        pb = jnp.concatenate([jnp.where(upper2, probs, 0.0), jnp.where(upper2, 0.0, probs)], axis=1).astype(BF16)
        o2 = jnp.dot(pb, v_band, preferred_element_type=F32)
        mix_ref[:, p * LANES:(p + 1) * LANES] = jnp.where(slot0, o2[:TILE], o2[TILE:]).astype(BF16)

    sc0 = scores(0)
    yield
    sc1 = scores(1)
    yield
    softmax_pv(0, sc0)
    yield
    sc2 = scores(2)
    yield
    softmax_pv(1, sc1)
    yield
    sc3 = scores(3)
    yield
    softmax_pv(2, sc2)
    yield

    x = jnp.dot(z_ref[:, Z_GLR:Z_GLR + LANES].astype(BF16), wup_ref[...], preferred_element_type=F32) + bg_ref[...]
    g = (jnp.minimum(x, 0.0) - jnp.log1p(jnp.exp(-jnp.abs(x)))) * (1.0 / GLA_GATE_TAU)
    g_hi = g.astype(BF16)
    g_lo = (g - g_hi.astype(F32)).astype(BF16)
    g_split = jnp.concatenate([g_hi, g_lo], axis=0)
    yield
    softmax_pv(3, sc3)
    yield
    decay = jnp.exp(jnp.dot(mstack_ref[...], g_split, preferred_element_type=F32))
    ones = jnp.ones((2 * TILE, LANES), BF16)
    tile_decay = jnp.exp(_dot_tn(g_split, ones))
    yield
    lid2 = lid_ref[...]

    def intra_scores(pp):
        sl = slice(pp * LANES, (pp + 1) * LANES)
        qp = z_ref[:, Z_GQ + pp * LANES:Z_GQ + (pp + 1) * LANES] * (GLA_DK ** -0.5)
        kp = z_ref[:, Z_GK + pp * LANES:Z_GK + (pp + 1) * LANES]
        q2 = jnp.concatenate([jnp.where(slot0, qp, 0.0), jnp.where(slot0, 0.0, qp)], axis=0)
        attn = _dot_nt(q2.astype(BF16), kp.astype(BF16))
        attn = jnp.where(lid2 == DIAG_LEVEL, attn, 0.0)
        for l in range(N_LEVELS):
            xl = decay[l * TILE:(l + 1) * TILE, sl]
            ql = (q2 * jnp.concatenate([xl, xl], axis=0)).astype(BF16)
            kl = (kp * xl).astype(BF16)
            attn = jnp.where(lid2 == l, _dot_nt(ql, kl), attn)
        xb = decay[N_LEVELS * TILE:(N_LEVELS + 1) * TILE, sl]
        xs = decay[(N_LEVELS + 1) * TILE:(N_LEVELS + 2) * TILE, sl]
        qe = (q2 * jnp.concatenate([xb, xb], axis=0)).astype(BF16)
        return attn.astype(BF16), qe, (kp * xs).astype(BF16)

    def outputs(pp, attn_b, qe, ks):
        st = state_ref[pp * LANES:(pp + 1) * LANES, :]
        stb = st.astype(BF16)
        v_wide = z_ref[:, Z_GV + 2 * pp * GLA_DV:Z_GV + 2 * (pp + 1) * GLA_DV].astype(BF16)
        for hh in range(2):
            h = 2 * pp + hh
            lhs = jnp.concatenate([attn_b[hh * TILE:(hh + 1) * TILE], qe[hh * TILE:(hh + 1) * TILE]], axis=1)
            rhs = jnp.concatenate([v_wide[:, hh * GLA_DV:(hh + 1) * GLA_DV], stb], axis=0)
            o = jnp.dot(lhs, rhs, preferred_element_type=F32)
            gog = z_ref[:, Z_GOG + h * GLA_DV:Z_GOG + (h + 1) * GLA_DV]
            o = _rmsnorm(o, gon_ref[...]) * (gog * jax.nn.sigmoid(gog))
            mix_ref[:, ATTN_WIDTH + h * GLA_DV:ATTN_WIDTH + (h + 1) * GLA_DV] = o.astype(BF16)
        upd = _dot_tn(ks, v_wide)
        td = tile_decay[pp * LANES:(pp + 1) * LANES, :]
        half = GLA_DK
        state_ref[pp * LANES:pp * LANES + half, :] = td[:half] * st[:half] + upd[:half, :GLA_DV]
        state_ref[pp * LANES + half:(pp + 1) * LANES, :] = td[half:] * st[half:] + upd[half:, GLA_DV:]

    intra0 = intra_scores(0)
    yield
    intra1 = intra_scores(1)
    yield
    outputs(0, *intra0)
    yield
    outputs(1, *intra1)
    yield


def _gla_constants():
    r = np.arange(TILE)[:, None]
    t = np.arange(TILE)[None, :]
    mats = []
    for l in range(N_LEVELS):
        size = (TILE // 2) >> l
        same = (r // size) == (t // size)
        odd = ((r // size) % 2) == 1
        mats.append(same & ((odd & (t <= r)) | (~odd & (t > r))))
    mats.append(t <= r)
    mats.append(t > r)
    mstack = np.concatenate(mats, axis=0).astype(np.float32)
    mstack = np.concatenate([mstack, mstack], axis=1)
    x = r ^ t
    top_bit = np.floor(np.log2(np.maximum(x, 1))).astype(np.int32)
    lid = np.where(t < r, (N_LEVELS - 1) - top_bit, np.where(t == r, DIAG_LEVEL, -1)).astype(np.int32)
    return jnp.asarray(mstack, BF16), jnp.asarray(np.concatenate([lid, lid], axis=0))


def _rope_tables(seq):
    dh = ATTN_HEAD_DIM
    inv_freq = ROPE_THETA ** (-jnp.arange(0, dh, 2, dtype=F32) / dh)
    ang = jnp.arange(seq).astype(F32)[:, None] * inv_freq[None, :]
    cos, sin = jnp.cos(ang), jnp.sin(ang)
    cos_t = jnp.tile(jnp.concatenate([cos, cos], axis=-1), (1, LANES // dh))
    sin_t = jnp.tile(jnp.concatenate([-sin, sin], axis=-1), (1, LANES // dh))
    return cos_t, sin_t


def _pair_perm():
    heads = np.stack([np.arange(4), np.arange(4) + 4], axis=1).reshape(-1)
    return (heads[:, None] * ATTN_HEAD_DIM + np.arange(ATTN_HEAD_DIM)[None, :]).reshape(-1)


def _prep_w_in(w_in):
    aq, ak, av, gq, gk, gv, glr, gog = jnp.split(
        w_in, np.cumsum([ATTN_WIDTH, ATTN_KV_WIDTH, ATTN_KV_WIDTH, GLA_QK_WIDTH, GLA_QK_WIDTH, GLA_WIDTH,
                         GLA_GATE_RANK]).tolist(), axis=-1)
    glr = jnp.pad(glr, ((0, 0), (0, LANES - GLA_GATE_RANK)))
    return jnp.concatenate([aq[:, _pair_perm()], ak, av, gq, gk, gv, gog, glr], axis=-1).astype(BF16)


def _resident(shape):
    return pl.BlockSpec(shape, lambda *_: (0,) * len(shape), pipeline_mode=pl.Buffered(1))


def _layer(h, p, ffn1_norm, ffn1_w_gate, ffn1_w_up, ffn1_w_down, mix_norm, w_in, q_norm, k_norm, attn_sinks,
           gla_w_gate_up, gla_b_gate, gla_out_norm, w_out, ffn2_norm, ffn2_w_gate, ffn2_w_up, ffn2_w_down,
           ple_gate_norm, ple_w_gate, ple_w_proj, ple_norm):
    batch, seq, d = h.shape
    tokens = batch * seq
    x2 = h.reshape(tokens, d)
    rows = lambda width: pl.BlockSpec((ROW_TILE, width), lambda i: (i, 0))
    vec = lambda v: v.reshape(1, -1).astype(F32)

    assert seq == BLOCKS_PER_SEQ * ROW_TILE
    nblk = tokens // ROW_TILE
    cos_t, sin_t = _rope_tables(seq)
    mstack, lid2 = _gla_constants()
    wup = jnp.pad(gla_w_gate_up, ((0, LANES - GLA_GATE_RANK), (0, 0))).astype(BF16)
    ffn_rows = pl.BlockSpec((ROW_TILE, d), lambda t: (jnp.minimum(t, nblk - 1), 0))
    mix_rows = pl.BlockSpec((ROW_TILE, d), lambda t: (jnp.maximum(t - 1, 0), 0))
    rope_rows = pl.BlockSpec((ROW_TILE, LANES), lambda t: (jnp.maximum(t - 1, 0) % BLOCKS_PER_SEQ, 0))
    h1, mix = pl.pallas_call(
        _ffn1_mixer_kernel,
        grid=(nblk + 1,),
        in_specs=[pl.BlockSpec(memory_space=pltpu.SMEM),
                  ffn_rows, _resident((1, d)), _resident((d, D_FF)), _resident((d, D_FF)), _resident((D_FF, d)),
                  _resident((1, d)), _resident((d, Z_WIDTH)),
                  rope_rows, rope_rows, _resident((1, LANES)), _resident((1, LANES)),
                  _resident((LANES, GLA_QK_WIDTH)), _resident((1, GLA_QK_WIDTH)), _resident((1, GLA_DV)),
                  _resident(((N_LEVELS + 2) * TILE, 2 * TILE)), _resident((2 * TILE, TILE))],
        out_specs=[ffn_rows, mix_rows],
        out_shape=[jax.ShapeDtypeStruct((tokens, d), F32), jax.ShapeDtypeStruct((tokens, d), BF16)],
        scratch_shapes=[pltpu.VMEM((ROW_TILE, D_MODEL), BF16), pltpu.VMEM((ROW_TILE, Z_WIDTH), F32),
                        pltpu.VMEM((TILE, LANES), BF16), pltpu.VMEM((TILE, LANES), BF16),
                        pltpu.VMEM((GLA_QK_WIDTH, GLA_DV), F32)],
        compiler_params=pltpu.CompilerParams(dimension_semantics=("arbitrary",), vmem_limit_bytes=VMEM_LIMIT,
                                             flags=FUSED_FLAGS),
        name="ffn1_mixer",
    )(attn_sinks.astype(F32), x2, vec(ffn1_norm), ffn1_w_gate.astype(BF16), ffn1_w_up.astype(BF16),
      ffn1_w_down.astype(BF16), vec(mix_norm), _prep_w_in(w_in),
      cos_t, sin_t, vec(jnp.tile(q_norm, 2)), vec(jnp.tile(k_norm, 2)), wup, vec(gla_b_gate), vec(gla_out_norm),
      mstack, lid2)

    wo = jnp.concatenate([w_out[:ATTN_WIDTH][_pair_perm()], w_out[ATTN_WIDTH:]], axis=0).astype(BF16)
    out = pl.pallas_call(
        _outproj_ffn2_ple_kernel,
        grid=(tokens // ROW_TILE,),
        in_specs=[rows(d), rows(d), rows(PLE_DIM), _resident((d, d)), _resident((1, d)), _resident((d, D_FF)),
                  _resident((d, D_FF)), _resident((D_FF, d)), _resident((1, d)), _resident((d, d)),
                  _resident((PLE_DIM, d)), _resident((1, d))],
        out_specs=rows(d),
        out_shape=jax.ShapeDtypeStruct((tokens, d), F32),
        compiler_params=pltpu.CompilerParams(dimension_semantics=("arbitrary",), vmem_limit_bytes=VMEM_LIMIT),
        name="outproj_ffn2_ple",
    )(mix, h1, p.reshape(tokens, PLE_DIM), wo, vec(ffn2_norm), ffn2_w_gate.astype(BF16), ffn2_w_up.astype(BF16),
      ffn2_w_down.astype(BF16), vec(ple_gate_norm), ple_w_gate.astype(BF16), ple_w_proj.astype(BF16),
      vec(ple_norm))
    return out.reshape(batch, seq, d)


@jax.jit
def kernel(x, p, ffn1_norm, ffn1_w_gate, ffn1_w_up, ffn1_w_down, mix_norm, w_in, q_norm, k_norm, attn_sinks,
           gla_w_gate_up, gla_b_gate, gla_out_norm, w_out, ffn2_norm, ffn2_w_gate, ffn2_w_up, ffn2_w_down,
           ple_gate_norm, ple_w_gate, ple_w_proj, ple_norm):
    h = x
    for i in range(p.shape[0]):
        h = _layer(h, p[i], ffn1_norm[i], ffn1_w_gate[i], ffn1_w_up[i], ffn1_w_down[i], mix_norm[i], w_in[i],
                   q_norm[i], k_norm[i], attn_sinks[i], gla_w_gate_up[i], gla_b_gate[i], gla_out_norm[i], w_out[i],
                   ffn2_norm[i], ffn2_w_gate[i], ffn2_w_up[i], ffn2_w_down[i], ple_gate_norm[i], ple_w_gate[i],
                   ple_w_proj[i], ple_norm[i])
    return h
```

```python
import itertools

import numpy as np
import jax
import jax.numpy as jnp
from jax import lax
from jax.experimental import pallas as pl
from jax.experimental.pallas import tpu as pltpu

F32 = jnp.float32
BF16 = jnp.bfloat16

D_MODEL = 1024
D_FF = 2816
PLE_DIM = 256
EPS = 1e-6

ATTN_HEAD_DIM = 64
ATTN_HEADS = 8
ATTN_KV_HEADS = 2
ATTN_WIDTH = ATTN_HEADS * ATTN_HEAD_DIM
ATTN_KV_WIDTH = ATTN_KV_HEADS * ATTN_HEAD_DIM
ROPE_THETA = 10000.0

GLA_HEADS = 4
GLA_DV = 128
GLA_DK = 64
GLA_QK_WIDTH = GLA_HEADS * GLA_DK
GLA_WIDTH = GLA_HEADS * GLA_DV
GLA_GATE_RANK = 16
GLA_GATE_TAU = 16.0

LANES = 128
TILE = 128
ROW_TILE = 512
BLOCKS_PER_SEQ = 8
MXU_DIM = 256
FF_CHUNKS = ((0, 6 * MXU_DIM), (6 * MXU_DIM, D_FF))
MIXER_STAGES = 8
VMEM_LIMIT = 56 * 1024 * 1024
FUSED_FLAGS = {}

Z_AQ, Z_AK, Z_AV = 0, 512, 640
Z_GQ, Z_GK, Z_GV, Z_GOG, Z_GLR = 768, 1024, 1280, 1792, 2304
Z_WIDTH = Z_GLR + LANES

N_LEVELS = 7
DIAG_LEVEL = N_LEVELS


def _rmsnorm(x, gain):
    return x * lax.rsqrt(jnp.mean(x * x, axis=-1, keepdims=True) + EPS) * gain


def _swiglu(nb, wg_ref, wu_ref, wd_ref, chunks=FF_CHUNKS, between=()):
    acc = None
    for c, (lo, hi) in enumerate(chunks):
        sl = slice(lo, hi)
        g = jnp.dot(nb, wg_ref[:, sl], preferred_element_type=F32)
        u = jnp.dot(nb, wu_ref[:, sl], preferred_element_type=F32)
        a = (g * jax.nn.sigmoid(g) * u).astype(BF16)
        d = jnp.dot(a, wd_ref[sl, :], preferred_element_type=F32)
        acc = d if acc is None else acc + d
        if c < len(between):
            between[c]()
    return acc


def _swiglu_units(nb, wg_ref, wu_ref, wd_ref, acc_box):
    acc = None
    for lo in range(0, D_FF, MXU_DIM):
        sl = slice(lo, lo + MXU_DIM)
        g = jnp.dot(nb, wg_ref[:, sl], preferred_element_type=F32)
        yield
        u = jnp.dot(nb, wu_ref[:, sl], preferred_element_type=F32)
        yield
        a = (g * jax.nn.sigmoid(g) * u).astype(BF16)
        d = jnp.dot(a, wd_ref[sl, :], preferred_element_type=F32)
        acc = d if acc is None else acc + d
        if lo + MXU_DIM < D_FF:
            yield
    acc_box.append(acc)
    yield


def _ffn1_mixer_kernel(sinks_ref, x_ref, n1_ref, wg_ref, wu_ref, wd_ref, n2_ref, win_ref,
                       cos_ref, sin_ref, qn_ref, kn_ref, wup_ref, bg_ref, gon_ref, mstack_ref, lid_ref,
                       h_ref, mix_ref, n_ref, z_ref, kprev_ref, vprev_ref, state_ref):
    t = pl.program_id(0)
    blk = jnp.maximum(t - 1, 0) % BLOCKS_PER_SEQ

    @pl.when(t == 0)
    def _():
        n_ref[...] = jnp.zeros_like(n_ref)

    @pl.when(blk == 0)
    def _():
        kprev_ref[...] = jnp.zeros_like(kprev_ref)
        vprev_ref[...] = jnp.zeros_like(vprev_ref)
        state_ref[...] = jnp.zeros_like(state_ref)

    z_ref[...] = jnp.dot(n_ref[...], win_ref[...], preferred_element_type=F32)

    def mixer_tile(j):
        rows = pl.ds(j * TILE, TILE)
        prev_ok = (blk > 0) if j == 0 else True
        return _mixer_tile(prev_ok, sinks_ref, z_ref.at[rows], cos_ref.at[rows], sin_ref.at[rows], qn_ref, kn_ref,
                           wup_ref, bg_ref, gon_ref, mstack_ref, lid_ref, mix_ref.at[rows], kprev_ref, vprev_ref,
                           state_ref)

    acc_box = []
    ffn = _swiglu_units(_rmsnorm(x_ref[...], n1_ref[...]).astype(BF16), wg_ref, wu_ref, wd_ref, acc_box)
    mixer = itertools.chain.from_iterable(mixer_tile(j) for j in range(ROW_TILE // TILE))
    n_mixer = (ROW_TILE // TILE) * MIXER_STAGES
    n_ffn = 3 * (D_FF // MXU_DIM)
    done = 0
    for i in range(n_mixer):
        next(mixer)
        while done * n_mixer < (i + 1) * n_ffn:
            next(ffn)
            done += 1
    assert done == n_ffn and next(ffn, None) is None and next(mixer, None) is None
    h = x_ref[...] + 0.5 * acc_box[0]
    h_ref[...] = h
    n_ref[...] = _rmsnorm(h, n2_ref[...]).astype(BF16)


def _outproj_ffn2_ple_kernel(mix_ref, h1_ref, p_ref, wo_ref, n1_ref, wg_ref, wu_ref, wd_ref,
                             gn_ref, wpg_ref, wpp_ref, pn_ref, out_ref):
    h2 = h1_ref[...] + jnp.dot(mix_ref[...], wo_ref[...], preferred_element_type=F32)
    emb = _rmsnorm(jnp.dot(p_ref[...].astype(BF16), wpp_ref[...], preferred_element_type=F32), pn_ref[...])
    h3 = h2 + 0.5 * _swiglu(_rmsnorm(h2, n1_ref[...]).astype(BF16), wg_ref, wu_ref, wd_ref)
    gate = jax.nn.sigmoid(jnp.dot(_rmsnorm(h3, gn_ref[...]).astype(BF16), wpg_ref[...],
                                  preferred_element_type=F32))
    out_ref[...] = h3 + gate * emb


def _dot_nt(a, b):
    return lax.dot_general(a, b, (((1,), (1,)), ((), ())), preferred_element_type=F32)


def _dot_tn(a, b):
    return lax.dot_general(a, b, (((0,), (0,)), ((), ())), preferred_element_type=F32)


def _mixer_tile(prev_ok, sinks_ref, z_ref, cos_ref, sin_ref, qn_ref, kn_ref, wup_ref, bg_ref, gon_ref,
                mstack_ref, lid_ref, mix_ref, kprev_ref, vprev_ref, state_ref):
    lane = lax.broadcasted_iota(jnp.int32, (TILE, LANES), 1)
    row = lax.broadcasted_iota(jnp.int32, (TILE, LANES), 0)
    slot0 = lane < ATTN_HEAD_DIM
    first_half = (lane % ATTN_HEAD_DIM) < (ATTN_HEAD_DIM // 2)
    upper = lane > row
    cos = cos_ref[...]
    sin = sin_ref[...]

    def qk_norm_rope(xp, gain):
        sq = xp * xp
        s0 = jnp.sum(jnp.where(slot0, sq, 0.0), axis=-1, keepdims=True)
        s1 = jnp.sum(jnp.where(slot0, 0.0, sq), axis=-1, keepdims=True)
        ms = jnp.where(slot0, s0, s1) * (1.0 / ATTN_HEAD_DIM)
        y = xp * lax.rsqrt(ms + EPS) * gain
        partner = jnp.where(first_half, pltpu.roll(y, LANES - ATTN_HEAD_DIM // 2, 1),
                            pltpu.roll(y, ATTN_HEAD_DIM // 2, 1))
        return y * cos + partner * sin

    kb = qk_norm_rope(z_ref[:, Z_AK:Z_AK + LANES], kn_ref[...]).astype(BF16)
    vb = z_ref[:, Z_AV:Z_AV + LANES].astype(BF16)
    k_band = jnp.concatenate([kprev_ref[...], kb], axis=0)
    v_band = jnp.concatenate([vprev_ref[...], vb], axis=0)
    kprev_ref[...] = kb
    vprev_ref[...] = vb
    upper2 = jnp.concatenate([upper, upper], axis=0)
    row2 = lax.broadcasted_iota(jnp.int32, (2 * TILE, 1), 0)

    def q_rows(p):
        qp = qk_norm_rope(z_ref[:, Z_AQ + p * LANES:Z_AQ + (p + 1) * LANES], qn_ref[...]) * (ATTN_HEAD_DIM ** -0.5)
        return jnp.concatenate([jnp.where(slot0, qp, 0.0), jnp.where(slot0, 0.0, qp)], axis=0).astype(BF16)

    def softmax(p, sc):
        s_prev = sc[:, :TILE] if prev_ok is True else jnp.where(prev_ok, sc[:, :TILE], -jnp.inf)
        sc = jnp.where(upper2, s_prev, sc[:, TILE:])
        sink = jnp.where(row2 < TILE, sinks_ref[p], sinks_ref[p + ATTN_HEADS // 2])
        m = jnp.maximum(jnp.max(sc, axis=-1, keepdims=True), sink)
        e = jnp.exp(sc - m)
        denom = jnp.sum(e, axis=-1, keepdims=True) + jnp.exp(sink - m)
        probs = e * (1.0 / denom)
        return jnp.concatenate([jnp.where(upper2, probs, 0.0), jnp.where(upper2, 0.0, probs)], axis=1).astype(BF16)

    pairs = range(ATTN_HEADS // 2)
    sc_all = _dot_nt(jnp.concatenate([q_rows(p) for p in pairs], axis=0), k_band)
    yield

    x = jnp.dot(z_ref[:, Z_GLR:Z_GLR + LANES].astype(BF16), wup_ref[...], preferred_element_type=F32) + bg_ref[...]
    g = (jnp.minimum(x, 0.0) - jnp.log1p(jnp.exp(-jnp.abs(x)))) * (1.0 / GLA_GATE_TAU)
    g_hi = g.astype(BF16)
    g_lo = (g - g_hi.astype(F32)).astype(BF16)
    g_split = jnp.concatenate([g_hi, g_lo], axis=0)
    yield
    decay = jnp.exp(jnp.dot(mstack_ref[...], g_split, preferred_element_type=F32))
    ones = jnp.ones((2 * TILE, LANES), BF16)
    tile_decay = jnp.exp(_dot_tn(g_split, ones))
    yield
    probs_all = jnp.concatenate([softmax(p, sc_all[2 * p * TILE:2 * (p + 1) * TILE]) for p in pairs], axis=0)
    o_all = jnp.dot(probs_all, v_band, preferred_element_type=F32)
    for p in pairs:
        o2 = o_all[2 * p * TILE:2 * (p + 1) * TILE]
        mix_ref[:, p * LANES:(p + 1) * LANES] = jnp.where(slot0, o2[:TILE], o2[TILE:]).astype(BF16)
    yield
    lid2 = lid_ref[...]

    def intra_scores(pp):
        sl = slice(pp * LANES, (pp + 1) * LANES)
        qp = z_ref[:, Z_GQ + pp * LANES:Z_GQ + (pp + 1) * LANES] * (GLA_DK ** -0.5)
        kp = z_ref[:, Z_GK + pp * LANES:Z_GK + (pp + 1) * LANES]
        q2 = jnp.concatenate([jnp.where(slot0, qp, 0.0), jnp.where(slot0, 0.0, qp)], axis=0)
        attn = _dot_nt(q2.astype(BF16), kp.astype(BF16))
        attn = jnp.where(lid2 == DIAG_LEVEL, attn, 0.0)
        for l in range(N_LEVELS):
            xl = decay[l * TILE:(l + 1) * TILE, sl]
            ql = (q2 * jnp.concatenate([xl, xl], axis=0)).astype(BF16)
            kl = (kp * xl).astype(BF16)
            attn = jnp.where(lid2 == l, _dot_nt(ql, kl), attn)
        xb = decay[N_LEVELS * TILE:(N_LEVELS + 1) * TILE, sl]
        xs = decay[(N_LEVELS + 1) * TILE:(N_LEVELS + 2) * TILE, sl]
        qe = (q2 * jnp.concatenate([xb, xb], axis=0)).astype(BF16)
        return attn.astype(BF16), qe, (kp * xs).astype(BF16)

    def outputs(pp, attn_b, qe, ks):
        st = state_ref[pp * LANES:(pp + 1) * LANES, :]
        stb = st.astype(BF16)
        v_wide = z_ref[:, Z_GV + 2 * pp * GLA_DV:Z_GV + 2 * (pp + 1) * GLA_DV].astype(BF16)
        for hh in range(2):
            h = 2 * pp + hh
            lhs = jnp.concatenate([attn_b[hh * TILE:(hh + 1) * TILE], qe[hh * TILE:(hh + 1) * TILE]], axis=1)
            rhs = jnp.concatenate([v_wide[:, hh * GLA_DV:(hh + 1) * GLA_DV], stb], axis=0)
            o = jnp.dot(lhs, rhs, preferred_element_type=F32)
            gog = z_ref[:, Z_GOG + h * GLA_DV:Z_GOG + (h + 1) * GLA_DV]
            o = _rmsnorm(o, gon_ref[...]) * (gog * jax.nn.sigmoid(gog))
            mix_ref[:, ATTN_WIDTH + h * GLA_DV:ATTN_WIDTH + (h + 1) * GLA_DV] = o.astype(BF16)
        upd = _dot_tn(ks, v_wide)
        td = tile_decay[pp * LANES:(pp + 1) * LANES, :]
        half = GLA_DK
        state_ref[pp * LANES:pp * LANES + half, :] = td[:half] * st[:half] + upd[:half, :GLA_DV]
        state_ref[pp * LANES + half:(pp + 1) * LANES, :] = td[half:] * st[half:] + upd[half:, GLA_DV:]

    intra0 = intra_scores(0)
    yield
    intra1 = intra_scores(1)
    yield
    outputs(0, *intra0)
    yield
    outputs(1, *intra1)
    yield


def _gla_constants():
    r = np.arange(TILE)[:, None]
    t = np.arange(TILE)[None, :]
    mats = []
    for l in range(N_LEVELS):
        size = (TILE // 2) >> l
        same = (r // size) == (t // size)
        odd = ((r // size) % 2) == 1
        mats.append(same & ((odd & (t <= r)) | (~odd & (t > r))))
    mats.append(t <= r)
    mats.append(t > r)
    mstack = np.concatenate(mats, axis=0).astype(np.float32)
    mstack = np.concatenate([mstack, mstack], axis=1)
    x = r ^ t
    top_bit = np.floor(np.log2(np.maximum(x, 1))).astype(np.int32)
    lid = np.where(t < r, (N_LEVELS - 1) - top_bit, np.where(t == r, DIAG_LEVEL, -1)).astype(np.int32)
    return jnp.asarray(mstack, BF16), jnp.asarray(np.concatenate([lid, lid], axis=0))


def _rope_tables(seq):
    dh = ATTN_HEAD_DIM
    inv_freq = ROPE_THETA ** (-jnp.arange(0, dh, 2, dtype=F32) / dh)
    ang = jnp.arange(seq).astype(F32)[:, None] * inv_freq[None, :]
    cos, sin = jnp.cos(ang), jnp.sin(ang)
    cos_t = jnp.tile(jnp.concatenate([cos, cos], axis=-1), (1, LANES // dh))
    sin_t = jnp.tile(jnp.concatenate([-sin, sin], axis=-1), (1, LANES // dh))
    return cos_t, sin_t


def _pair_perm():
    heads = np.stack([np.arange(4), np.arange(4) + 4], axis=1).reshape(-1)
    return (heads[:, None] * ATTN_HEAD_DIM + np.arange(ATTN_HEAD_DIM)[None, :]).reshape(-1)


def _prep_w_in(w_in):
    aq, ak, av, gq, gk, gv, glr, gog = jnp.split(
        w_in, np.cumsum([ATTN_WIDTH, ATTN_KV_WIDTH, ATTN_KV_WIDTH, GLA_QK_WIDTH, GLA_QK_WIDTH, GLA_WIDTH,
                         GLA_GATE_RANK]).tolist(), axis=-1)
    glr = jnp.pad(glr, ((0, 0), (0, LANES - GLA_GATE_RANK)))
    return jnp.concatenate([aq[:, _pair_perm()], ak, av, gq, gk, gv, gog, glr], axis=-1).astype(BF16)


def _resident(shape):
    return pl.BlockSpec(shape, lambda *_: (0,) * len(shape), pipeline_mode=pl.Buffered(1))


def _layer(h, p, ffn1_norm, ffn1_w_gate, ffn1_w_up, ffn1_w_down, mix_norm, w_in, q_norm, k_norm, attn_sinks,
           gla_w_gate_up, gla_b_gate, gla_out_norm, w_out, ffn2_norm, ffn2_w_gate, ffn2_w_up, ffn2_w_down,
           ple_gate_norm, ple_w_gate, ple_w_proj, ple_norm):
    batch, seq, d = h.shape
    tokens = batch * seq
    x2 = h.reshape(tokens, d)
    rows = lambda width: pl.BlockSpec((ROW_TILE, width), lambda i: (i, 0))
    vec = lambda v: v.reshape(1, -1).astype(F32)

    assert seq == BLOCKS_PER_SEQ * ROW_TILE
    nblk = tokens // ROW_TILE
    cos_t, sin_t = _rope_tables(seq)
    mstack, lid2 = _gla_constants()
    wup = jnp.pad(gla_w_gate_up, ((0, LANES - GLA_GATE_RANK), (0, 0))).astype(BF16)
    ffn_rows = pl.BlockSpec((ROW_TILE, d), lambda t: (jnp.minimum(t, nblk - 1), 0))
    mix_rows = pl.BlockSpec((ROW_TILE, d), lambda t: (jnp.maximum(t - 1, 0), 0))
    rope_rows = pl.BlockSpec((ROW_TILE, LANES), lambda t: (jnp.maximum(t - 1, 0) % BLOCKS_PER_SEQ, 0))
    h1, mix = pl.pallas_call(
        _ffn1_mixer_kernel,
        grid=(nblk + 1,),
        in_specs=[pl.BlockSpec(memory_space=pltpu.SMEM),
                  ffn_rows, _resident((1, d)), _resident((d, D_FF)), _resident((d, D_FF)), _resident((D_FF, d)),
                  _resident((1, d)), _resident((d, Z_WIDTH)),
                  rope_rows, rope_rows, _resident((1, LANES)), _resident((1, LANES)),
                  _resident((LANES, GLA_QK_WIDTH)), _resident((1, GLA_QK_WIDTH)), _resident((1, GLA_DV)),
                  _resident(((N_LEVELS + 2) * TILE, 2 * TILE)), _resident((2 * TILE, TILE))],
        out_specs=[ffn_rows, mix_rows],
        out_shape=[jax.ShapeDtypeStruct((tokens, d), F32), jax.ShapeDtypeStruct((tokens, d), BF16)],
        scratch_shapes=[pltpu.VMEM((ROW_TILE, D_MODEL), BF16), pltpu.VMEM((ROW_TILE, Z_WIDTH), F32),
                        pltpu.VMEM((TILE, LANES), BF16), pltpu.VMEM((TILE, LANES), BF16),
                        pltpu.VMEM((GLA_QK_WIDTH, GLA_DV), F32)],
        compiler_params=pltpu.CompilerParams(dimension_semantics=("arbitrary",), vmem_limit_bytes=VMEM_LIMIT,
                                             flags=FUSED_FLAGS),
        name="ffn1_mixer",
    )(attn_sinks.astype(F32), x2, vec(ffn1_norm), ffn1_w_gate.astype(BF16), ffn1_w_up.astype(BF16),
      ffn1_w_down.astype(BF16), vec(mix_norm), _prep_w_in(w_in),
      cos_t, sin_t, vec(jnp.tile(q_norm, 2)), vec(jnp.tile(k_norm, 2)), wup, vec(gla_b_gate), vec(gla_out_norm),
      mstack, lid2)

    wo = jnp.concatenate([w_out[:ATTN_WIDTH][_pair_perm()], w_out[ATTN_WIDTH:]], axis=0).astype(BF16)
    out = pl.pallas_call(
        _outproj_ffn2_ple_kernel,
        grid=(tokens // ROW_TILE,),
        in_specs=[rows(d), rows(d), rows(PLE_DIM), _resident((d, d)), _resident((1, d)), _resident((d, D_FF)),
                  _resident((d, D_FF)), _resident((D_FF, d)), _resident((1, d)), _resident((d, d)),
                  _resident((PLE_DIM, d)), _resident((1, d))],
        out_specs=rows(d),
        out_shape=jax.ShapeDtypeStruct((tokens, d), F32),
        compiler_params=pltpu.CompilerParams(dimension_semantics=("arbitrary",), vmem_limit_bytes=VMEM_LIMIT),
        name="outproj_ffn2_ple",
    )(mix, h1, p.reshape(tokens, PLE_DIM), wo, vec(ffn2_norm), ffn2_w_gate.astype(BF16), ffn2_w_up.astype(BF16),
      ffn2_w_down.astype(BF16), vec(ple_gate_norm), ple_w_gate.astype(BF16), ple_w_proj.astype(BF16),
      vec(ple_norm))
    return out.reshape(batch, seq, d)


@jax.jit
def kernel(x, p, ffn1_norm, ffn1_w_gate, ffn1_w_up, ffn1_w_down, mix_norm, w_in, q_norm, k_norm, attn_sinks,
           gla_w_gate_up, gla_b_gate, gla_out_norm, w_out, ffn2_norm, ffn2_w_gate, ffn2_w_up, ffn2_w_down,
           ple_gate_norm, ple_w_gate, ple_w_proj, ple_norm):
    h = x
    for i in range(p.shape[0]):
        h = _layer(h, p[i], ffn1_norm[i], ffn1_w_gate[i], ffn1_w_up[i], ffn1_w_down[i], mix_norm[i], w_in[i],
                   q_norm[i], k_norm[i], attn_sinks[i], gla_w_gate_up[i], gla_b_gate[i], gla_out_norm[i], w_out[i],
                   ffn2_norm[i], ffn2_w_gate[i], ffn2_w_up[i], ffn2_w_down[i], ple_gate_norm[i], ple_w_gate[i],
                   ple_w_proj[i], ple_norm[i])
    return h
```

```python
import numpy as np
import jax
import jax.numpy as jnp
from jax import lax
from jax.experimental import pallas as pl
from jax.experimental.pallas import tpu as pltpu

F32 = jnp.float32
BF16 = jnp.bfloat16

D_MODEL = 1024
D_FF = 2816
PLE_DIM = 256
EPS = 1e-6

ATTN_HEAD_DIM = 64
ATTN_HEADS = 8
ATTN_KV_HEADS = 2
ATTN_WIDTH = ATTN_HEADS * ATTN_HEAD_DIM
ATTN_KV_WIDTH = ATTN_KV_HEADS * ATTN_HEAD_DIM
ROPE_THETA = 10000.0

GLA_HEADS = 4
GLA_DV = 128
GLA_DK = 64
GLA_QK_WIDTH = GLA_HEADS * GLA_DK
GLA_WIDTH = GLA_HEADS * GLA_DV
GLA_GATE_RANK = 16
GLA_GATE_TAU = 16.0

LANES = 128
TILE = 128
ROW_TILE = 512
BLOCKS_PER_SEQ = 8
MXU_DIM = 256
FF_CHUNKS = ((0, 6 * MXU_DIM), (6 * MXU_DIM, D_FF))
MIXER_STAGES = 8
TILE_SKEW = 5
VMEM_LIMIT = 56 * 1024 * 1024
FUSED_FLAGS = {}

Z_AQ, Z_AK, Z_AV = 0, 512, 640
Z_GQ, Z_GK, Z_GV, Z_GOG, Z_GLR = 768, 1024, 1280, 1792, 2304
Z_WIDTH = Z_GLR + LANES

N_LEVELS = 7
DIAG_LEVEL = N_LEVELS
FINE_LEVELS = 3


def _rmsnorm(x, gain):
    return x * lax.rsqrt(jnp.mean(x * x, axis=-1, keepdims=True) + EPS) * gain


def _swiglu(nb, wg_ref, wu_ref, wd_ref, chunks=FF_CHUNKS, between=()):
    acc = None
    for c, (lo, hi) in enumerate(chunks):
        sl = slice(lo, hi)
        g = jnp.dot(nb, wg_ref[:, sl], preferred_element_type=F32)
        u = jnp.dot(nb, wu_ref[:, sl], preferred_element_type=F32)
        a = (g * jax.nn.sigmoid(g) * u).astype(BF16)
        d = jnp.dot(a, wd_ref[sl, :], preferred_element_type=F32)
        acc = d if acc is None else acc + d
        if c < len(between):
            between[c]()
    return acc


def _swiglu_units(nb, wg_ref, wu_ref, wd_ref, acc_box):
    acc = None
    for lo in range(0, D_FF, MXU_DIM):
        sl = slice(lo, lo + MXU_DIM)
        g = jnp.dot(nb, wg_ref[:, sl], preferred_element_type=F32)
        yield
        u = jnp.dot(nb, wu_ref[:, sl], preferred_element_type=F32)
        yield
        a = (g * jax.nn.sigmoid(g) * u).astype(BF16)
        d = jnp.dot(a, wd_ref[sl, :], preferred_element_type=F32)
        acc = d if acc is None else acc + d
        if lo + MXU_DIM < D_FF:
            yield
    acc_box.append(acc)
    yield


def _ffn1_mixer_kernel(sinks_ref, x_ref, n1_ref, wg_ref, wu_ref, wd_ref, n2_ref, win_ref,
                       cos_ref, sin_ref, qn_ref, kn_ref, wup_ref, bg_ref, gon_ref, tri_ref, mfine_ref, lid_ref,
                       h_ref, mix_ref, n_ref, z_ref, kprev_ref, vprev_ref, state_ref):
    t = pl.program_id(0)
    blk = jnp.maximum(t - 1, 0) % BLOCKS_PER_SEQ

    @pl.when(t == 0)
    def _():
        n_ref[...] = jnp.zeros_like(n_ref)

    @pl.when(blk == 0)
    def _():
        kprev_ref[...] = jnp.zeros_like(kprev_ref)
        vprev_ref[...] = jnp.zeros_like(vprev_ref)
        state_ref[...] = jnp.zeros_like(state_ref)

    z_ref[...] = jnp.dot(n_ref[...], win_ref[...], preferred_element_type=F32)

    def mixer_tile(j):
        rows = pl.ds(j * TILE, TILE)
        prev_ok = (blk > 0) if j == 0 else True
        return _mixer_tile(prev_ok, sinks_ref, z_ref.at[rows], cos_ref.at[rows], sin_ref.at[rows], qn_ref, kn_ref,
                           wup_ref, bg_ref, gon_ref, tri_ref, mfine_ref, lid_ref, mix_ref.at[rows], kprev_ref, vprev_ref,
                           state_ref)

    acc_box = []
    ffn = _swiglu_units(_rmsnorm(x_ref[...], n1_ref[...]).astype(BF16), wg_ref, wu_ref, wd_ref, acc_box)
    tiles = [mixer_tile(j) for j in range(ROW_TILE // TILE)]
    order = sorted((j * TILE_SKEW + s, j) for j in range(len(tiles)) for s in range(MIXER_STAGES))
    n_mixer = len(order)
    n_ffn = 3 * (D_FF // MXU_DIM)
    done = 0
    for i, (_, j) in enumerate(order):
        next(tiles[j])
        while done * n_mixer < (i + 1) * n_ffn:
            next(ffn)
            done += 1
    assert done == n_ffn and next(ffn, None) is None and all(next(tile, None) is None for tile in tiles)
    h = x_ref[...] + 0.5 * acc_box[0]
    h_ref[...] = h
    n_ref[...] = _rmsnorm(h, n2_ref[...]).astype(BF16)


def _outproj_ffn2_ple_kernel(mix_ref, h1_ref, p_ref, wo_ref, n1_ref, wg_ref, wu_ref, wd_ref,
                             gn_ref, wpg_ref, wpp_ref, pn_ref, out_ref):
    h2 = h1_ref[...] + jnp.dot(mix_ref[...], wo_ref[...], preferred_element_type=F32)
    emb = _rmsnorm(jnp.dot(p_ref[...].astype(BF16), wpp_ref[...], preferred_element_type=F32), pn_ref[...])
    h3 = h2 + 0.5 * _swiglu(_rmsnorm(h2, n1_ref[...]).astype(BF16), wg_ref, wu_ref, wd_ref)
    gate = jax.nn.sigmoid(jnp.dot(_rmsnorm(h3, gn_ref[...]).astype(BF16), wpg_ref[...],
                                  preferred_element_type=F32))
    out_ref[...] = h3 + gate * emb


def _dot_nt(a, b):
    return lax.dot_general(a, b, (((1,), (1,)), ((), ())), preferred_element_type=F32)


def _dot_tn(a, b):
    return lax.dot_general(a, b, (((0,), (0,)), ((), ())), preferred_element_type=F32)


def _mixer_tile(prev_ok, sinks_ref, z_ref, cos_ref, sin_ref, qn_ref, kn_ref, wup_ref, bg_ref, gon_ref,
                tri_ref, mfine_ref, lid_ref, mix_ref, kprev_ref, vprev_ref, state_ref):
    lane = lax.broadcasted_iota(jnp.int32, (TILE, LANES), 1)
    row = lax.broadcasted_iota(jnp.int32, (TILE, LANES), 0)
    slot0 = lane < ATTN_HEAD_DIM
    first_half = (lane % ATTN_HEAD_DIM) < (ATTN_HEAD_DIM // 2)
    upper = lane > row
    cos = cos_ref[...]
    sin = sin_ref[...]

    def qk_norm_rope(xp, gain):
        sq = xp * xp
        s0 = jnp.sum(jnp.where(slot0, sq, 0.0), axis=-1, keepdims=True)
        s1 = jnp.sum(jnp.where(slot0, 0.0, sq), axis=-1, keepdims=True)
        ms = jnp.where(slot0, s0, s1) * (1.0 / ATTN_HEAD_DIM)
        y = xp * lax.rsqrt(ms + EPS) * gain
        partner = jnp.where(first_half, pltpu.roll(y, LANES - ATTN_HEAD_DIM // 2, 1),
                            pltpu.roll(y, ATTN_HEAD_DIM // 2, 1))
        return y * cos + partner * sin

    kb = qk_norm_rope(z_ref[:, Z_AK:Z_AK + LANES], kn_ref[...]).astype(BF16)
    vb = z_ref[:, Z_AV:Z_AV + LANES].astype(BF16)
    k_band = jnp.concatenate([kprev_ref[...], kb], axis=0)
    v_band = jnp.concatenate([vprev_ref[...], vb], axis=0)
    kprev_ref[...] = kb
    vprev_ref[...] = vb
    upper2 = jnp.concatenate([upper, upper], axis=0)
    row2 = lax.broadcasted_iota(jnp.int32, (2 * TILE, 1), 0)

    def q_rows(p):
        qp = qk_norm_rope(z_ref[:, Z_AQ + p * LANES:Z_AQ + (p + 1) * LANES], qn_ref[...]) * (ATTN_HEAD_DIM ** -0.5)
        return jnp.concatenate([jnp.where(slot0, qp, 0.0), jnp.where(slot0, 0.0, qp)], axis=0).astype(BF16)

    def softmax(p, sc):
        s_prev = sc[:, :TILE] if prev_ok is True else jnp.where(prev_ok, sc[:, :TILE], -jnp.inf)
        sc = jnp.where(upper2, s_prev, sc[:, TILE:])
        sink = jnp.where(row2 < TILE, sinks_ref[p], sinks_ref[p + ATTN_HEADS // 2])
        m = jnp.maximum(jnp.max(sc, axis=-1, keepdims=True), sink)
        e = jnp.exp(sc - m)
        denom = jnp.sum(e, axis=-1, keepdims=True) + jnp.exp(sink - m)
        probs = e * (1.0 / denom)
        return jnp.concatenate([jnp.where(upper2, probs, 0.0), jnp.where(upper2, 0.0, probs)], axis=1).astype(BF16)

    pairs = range(ATTN_HEADS // 2)

    x = jnp.dot(z_ref[:, Z_GLR:Z_GLR + LANES].astype(BF16), wup_ref[...], preferred_element_type=F32) + bg_ref[...]
    g = (jnp.minimum(x, 0.0) - jnp.log1p(jnp.exp(-jnp.abs(x)))) * (1.0 / GLA_GATE_TAU)
    g_hi = g.astype(BF16)
    r1 = g - g_hi.astype(F32)
    g_mid = r1.astype(BF16)
    g_lo = (r1 - g_mid.astype(F32)).astype(BF16)
    g_split2 = jnp.concatenate([g_hi, g_mid], axis=0)
    g_split3 = jnp.concatenate([g_hi, g_mid, g_lo], axis=0)
    yield
    sc_all = _dot_nt(jnp.concatenate([q_rows(p) for p in pairs], axis=0), k_band)
    yield
    b = jnp.dot(tri_ref[...], g_split3, preferred_element_type=F32)
    fine = jnp.exp(jnp.dot(mfine_ref[...], g_split2, preferred_element_type=F32))

    def level_decay(l):
        if l >= N_LEVELS - FINE_LEVELS:
            k = l - (N_LEVELS - FINE_LEVELS)
            return fine[k * TILE:(k + 1) * TILE]
        size = (TILE // 2) >> l
        parts = []
        for m in range(TILE // (2 * size)):
            lo, piv = 2 * m * size, (2 * m + 1) * size - 1
            bp = b[piv:piv + 1]
            parts += [bp - b[lo:piv + 1], b[piv + 1:piv + 1 + size] - bp]
        return jnp.exp(jnp.concatenate(parts, axis=0))

    decays = [level_decay(l) for l in range(N_LEVELS)]
    decay_b = jnp.exp(b)
    decay_s = jnp.exp(b[TILE - 1:TILE] - b)
    ones = jnp.ones((2 * TILE, LANES), BF16)
    tile_decay = jnp.exp(_dot_tn(g_split2, ones))
    yield
    probs_all = jnp.concatenate([softmax(p, sc_all[2 * p * TILE:2 * (p + 1) * TILE]) for p in pairs], axis=0)
    o_all = jnp.dot(probs_all, v_band, preferred_element_type=F32)
    for p in pairs:
        o2 = o_all[2 * p * TILE:2 * (p + 1) * TILE]
        mix_ref[:, p * LANES:(p + 1) * LANES] = jnp.where(slot0, o2[:TILE], o2[TILE:]).astype(BF16)
    yield
    lid2 = lid_ref[...]

    def intra_scores(pp):
        sl = slice(pp * LANES, (pp + 1) * LANES)
        qp = z_ref[:, Z_GQ + pp * LANES:Z_GQ + (pp + 1) * LANES] * (GLA_DK ** -0.5)
        kp = z_ref[:, Z_GK + pp * LANES:Z_GK + (pp + 1) * LANES]
        q2 = jnp.concatenate([jnp.where(slot0, qp, 0.0), jnp.where(slot0, 0.0, qp)], axis=0)
        qk = qp * kp
        diag = jnp.concatenate([jnp.sum(jnp.where(slot0, qk, 0.0), axis=-1, keepdims=True),
                                jnp.sum(jnp.where(slot0, 0.0, qk), axis=-1, keepdims=True)], axis=0)
        attn = jnp.where(lid2 == DIAG_LEVEL, diag, 0.0)
        for l in range(N_LEVELS):
            xl = decays[l][:, sl]
            ql = (q2 * jnp.concatenate([xl, xl], axis=0)).astype(BF16)
            kl = (kp * xl).astype(BF16)
            attn = jnp.where(lid2 == l, _dot_nt(ql, kl), attn)
        xb = decay_b[:, sl]
        xs = decay_s[:, sl]
        qe = (q2 * jnp.concatenate([xb, xb], axis=0)).astype(BF16)
        return attn.astype(BF16), qe, (kp * xs).astype(BF16)

    def outputs(pp, attn_b, qe, ks):
        st = state_ref[pp * LANES:(pp + 1) * LANES, :]
        stb = st.astype(BF16)
        v_wide = z_ref[:, Z_GV + 2 * pp * GLA_DV:Z_GV + 2 * (pp + 1) * GLA_DV].astype(BF16)
        for hh in range(2):
            h = 2 * pp + hh
            lhs = jnp.concatenate([attn_b[hh * TILE:(hh + 1) * TILE], qe[hh * TILE:(hh + 1) * TILE]], axis=1)
            rhs = jnp.concatenate([v_wide[:, hh * GLA_DV:(hh + 1) * GLA_DV], stb], axis=0)
            o = jnp.dot(lhs, rhs, preferred_element_type=F32)
            gog = z_ref[:, Z_GOG + h * GLA_DV:Z_GOG + (h + 1) * GLA_DV]
            o = _rmsnorm(o, gon_ref[...]) * (gog * jax.nn.sigmoid(gog))
            mix_ref[:, ATTN_WIDTH + h * GLA_DV:ATTN_WIDTH + (h + 1) * GLA_DV] = o.astype(BF16)
        upd = _dot_tn(ks, v_wide)
        td = tile_decay[pp * LANES:(pp + 1) * LANES, :]
        half = GLA_DK
        state_ref[pp * LANES:pp * LANES + half, :] = td[:half] * st[:half] + upd[:half, :GLA_DV]
        state_ref[pp * LANES + half:(pp + 1) * LANES, :] = td[half:] * st[half:] + upd[half:, GLA_DV:]

    intra0 = intra_scores(0)
    yield
    intra1 = intra_scores(1)
    yield
    outputs(0, *intra0)
    yield
    outputs(1, *intra1)
    yield


def _gla_constants():
    r = np.arange(TILE)[:, None]
    t = np.arange(TILE)[None, :]
    mats = []
    for l in range(N_LEVELS - FINE_LEVELS, N_LEVELS):
        size = (TILE // 2) >> l
        same = (r // size) == (t // size)
        odd = ((r // size) % 2) == 1
        mats.append(same & ((odd & (t <= r)) | (~odd & (t > r))))
    mfine = np.concatenate(mats, axis=0).astype(np.float32)
    mfine = np.concatenate([mfine, mfine], axis=1)
    tri = np.concatenate([(t <= r).astype(np.float32)] * 3, axis=1)
    x = r ^ t
    top_bit = np.floor(np.log2(np.maximum(x, 1))).astype(np.int32)
    lid = np.where(t < r, (N_LEVELS - 1) - top_bit, np.where(t == r, DIAG_LEVEL, -1)).astype(np.int32)
    return jnp.asarray(tri, BF16), jnp.asarray(mfine, BF16), jnp.asarray(np.concatenate([lid, lid], axis=0))


def _rope_tables(seq):
    dh = ATTN_HEAD_DIM
    inv_freq = ROPE_THETA ** (-jnp.arange(0, dh, 2, dtype=F32) / dh)
    ang = jnp.arange(seq).astype(F32)[:, None] * inv_freq[None, :]
    cos, sin = jnp.cos(ang), jnp.sin(ang)
    cos_t = jnp.tile(jnp.concatenate([cos, cos], axis=-1), (1, LANES // dh))
    sin_t = jnp.tile(jnp.concatenate([-sin, sin], axis=-1), (1, LANES // dh))
    return cos_t, sin_t


def _prep_w_in(w_in):
    rows = w_in.shape[0]
    mid = ATTN_WIDTH + 2 * ATTN_KV_WIDTH + 2 * GLA_QK_WIDTH + GLA_WIDTH
    aq = w_in[:, :ATTN_WIDTH].astype(BF16).reshape(rows, 2, ATTN_HEADS // 2, ATTN_HEAD_DIM)
    aq = aq.transpose(0, 2, 1, 3).reshape(rows, ATTN_WIDTH)
    glr = jnp.pad(w_in[:, mid:mid + GLA_GATE_RANK].astype(BF16), ((0, 0), (0, LANES - GLA_GATE_RANK)))
    return jnp.concatenate([aq, w_in[:, ATTN_WIDTH:mid].astype(BF16), w_in[:, mid + GLA_GATE_RANK:].astype(BF16),
                            glr], axis=-1)


def _resident(shape):
    return pl.BlockSpec(shape, lambda *_: (0,) * len(shape), pipeline_mode=pl.Buffered(1))


def _layer(layer, h, p, ffn1_norm, ffn1_w_gate, ffn1_w_up, ffn1_w_down, mix_norm, w_in, q_norm, k_norm, attn_sinks,
           gla_w_gate_up, gla_b_gate, gla_out_norm, w_out, ffn2_norm, ffn2_w_gate, ffn2_w_up, ffn2_w_down,
           ple_gate_norm, ple_w_gate, ple_w_proj, ple_norm):
    batch, seq, d = h.shape
    tokens = batch * seq
    x2 = h.reshape(tokens, d)
    rows = lambda width: pl.BlockSpec((ROW_TILE, width), lambda i: (i, 0))
    vec = lambda v: v.reshape(1, -1).astype(F32)

    assert seq == BLOCKS_PER_SEQ * ROW_TILE
    nblk = tokens // ROW_TILE
    cos_t, sin_t = _rope_tables(seq)
    tri, mfine, lid2 = _gla_constants()
    wup = jnp.pad(gla_w_gate_up, ((0, LANES - GLA_GATE_RANK), (0, 0))).astype(BF16)
    ffn_rows = pl.BlockSpec((ROW_TILE, d), lambda t: (jnp.minimum(t, nblk - 1), 0))
    mix_rows = pl.BlockSpec((ROW_TILE, d), lambda t: (jnp.maximum(t - 1, 0), 0))
    rope_rows = pl.BlockSpec((ROW_TILE, LANES), lambda t: (jnp.maximum(t - 1, 0) % BLOCKS_PER_SEQ, 0))
    h1, mix = pl.pallas_call(
        _ffn1_mixer_kernel,
        grid=(nblk + 1,),
        in_specs=[pl.BlockSpec(memory_space=pltpu.SMEM),
                  ffn_rows, _resident((1, d)), _resident((d, D_FF)), _resident((d, D_FF)), _resident((D_FF, d)),
                  _resident((1, d)), _resident((d, Z_WIDTH)),
                  rope_rows, rope_rows, _resident((1, LANES)), _resident((1, LANES)),
                  _resident((LANES, GLA_QK_WIDTH)), _resident((1, GLA_QK_WIDTH)), _resident((1, GLA_DV)),
                  _resident((TILE, 3 * TILE)), _resident((FINE_LEVELS * TILE, 2 * TILE)),
                  _resident((2 * TILE, TILE))],
        out_specs=[ffn_rows, mix_rows],
        out_shape=[jax.ShapeDtypeStruct((tokens, d), F32), jax.ShapeDtypeStruct((tokens, d), BF16)],
        scratch_shapes=[pltpu.VMEM((ROW_TILE, D_MODEL), BF16), pltpu.VMEM((ROW_TILE, Z_WIDTH), F32),
                        pltpu.VMEM((TILE, LANES), BF16), pltpu.VMEM((TILE, LANES), BF16),
                        pltpu.VMEM((GLA_QK_WIDTH, GLA_DV), F32)],
        compiler_params=pltpu.CompilerParams(dimension_semantics=("arbitrary",), vmem_limit_bytes=VMEM_LIMIT,
                                             flags=FUSED_FLAGS),
        name="ffn1_mixer",
    )(attn_sinks.astype(F32), x2, vec(ffn1_norm), ffn1_w_gate.astype(BF16), ffn1_w_up.astype(BF16),
      ffn1_w_down.astype(BF16), vec(mix_norm), _prep_w_in(w_in),
      cos_t, sin_t, vec(jnp.tile(q_norm, 2)), vec(jnp.tile(k_norm, 2)), wup, vec(gla_b_gate), vec(gla_out_norm),
      tri, mfine, lid2)

    wo_attn = w_out[:ATTN_WIDTH].astype(BF16).reshape(2, ATTN_HEADS // 2, ATTN_HEAD_DIM, d)
    wo = jnp.concatenate([wo_attn.transpose(1, 0, 2, 3).reshape(ATTN_WIDTH, d), w_out[ATTN_WIDTH:].astype(BF16)],
                         axis=0)
    out = pl.pallas_call(
        _outproj_ffn2_ple_kernel,
        grid=(tokens // ROW_TILE,),
        in_specs=[rows(d), rows(d), pl.BlockSpec((None, ROW_TILE, PLE_DIM), lambda r: (layer, r, 0)),
                  _resident((d, d)), _resident((1, d)), _resident((d, D_FF)),
                  _resident((d, D_FF)), _resident((D_FF, d)), _resident((1, d)), _resident((d, d)),
                  _resident((PLE_DIM, d)), _resident((1, d))],
        out_specs=rows(d),
        out_shape=jax.ShapeDtypeStruct((tokens, d), F32),
        compiler_params=pltpu.CompilerParams(dimension_semantics=("arbitrary",), vmem_limit_bytes=VMEM_LIMIT),
        name="outproj_ffn2_ple",
    )(mix, h1, p.reshape(p.shape[0], tokens, PLE_DIM), wo, vec(ffn2_norm), ffn2_w_gate.astype(BF16), ffn2_w_up.astype(BF16),
      ffn2_w_down.astype(BF16), vec(ple_gate_norm), ple_w_gate.astype(BF16), ple_w_proj.astype(BF16),
      vec(ple_norm))
    return out.reshape(batch, seq, d)


@jax.jit
def kernel(x, p, ffn1_norm, ffn1_w_gate, ffn1_w_up, ffn1_w_down, mix_norm, w_in, q_norm, k_norm, attn_sinks,
           gla_w_gate_up, gla_b_gate, gla_out_norm, w_out, ffn2_norm, ffn2_w_gate, ffn2_w_up, ffn2_w_down,
           ple_gate_norm, ple_w_gate, ple_w_proj, ple_norm):
    h = x
    for i in range(p.shape[0]):
        h = _layer(i, h, p, ffn1_norm[i], ffn1_w_gate[i], ffn1_w_up[i], ffn1_w_down[i], mix_norm[i], w_in[i],
                   q_norm[i], k_norm[i], attn_sinks[i], gla_w_gate_up[i], gla_b_gate[i], gla_out_norm[i], w_out[i],
                   ffn2_norm[i], ffn2_w_gate[i], ffn2_w_up[i], ffn2_w_down[i], ple_gate_norm[i], ple_w_gate[i],
                   ple_w_proj[i], ple_norm[i])
    return h
```

```python
import numpy as np
import jax
import jax.numpy as jnp
from jax import lax
from jax.experimental import pallas as pl
from jax.experimental.pallas import tpu as pltpu

F32 = jnp.float32
BF16 = jnp.bfloat16

D_MODEL = 1024
D_FF = 2816
PLE_DIM = 256
EPS = 1e-6

ATTN_HEAD_DIM = 64
ATTN_HEADS = 8
ATTN_KV_HEADS = 2
ATTN_WIDTH = ATTN_HEADS * ATTN_HEAD_DIM
ATTN_KV_WIDTH = ATTN_KV_HEADS * ATTN_HEAD_DIM
ROPE_THETA = 10000.0

GLA_HEADS = 4
GLA_DV = 128
GLA_DK = 64
GLA_QK_WIDTH = GLA_HEADS * GLA_DK
GLA_WIDTH = GLA_HEADS * GLA_DV
GLA_GATE_RANK = 16
GLA_GATE_TAU = 16.0

LANES = 128
TILE = 128
ROW_TILE = 512
BLOCKS_PER_SEQ = 8
MXU_DIM = 256
FF_CHUNKS = tuple((lo, min(lo + 3 * MXU_DIM, D_FF)) for lo in range(0, D_FF, 3 * MXU_DIM))
MIXER_STAGES = 8
TILE_SKEW = 5
VMEM_LIMIT = 56 * 1024 * 1024
FUSED_FLAGS = {}

Z_AQ, Z_AK, Z_AV = 0, 512, 640
Z_GQ, Z_GK, Z_GV, Z_GOG, Z_GLR = 768, 1024, 1280, 1792, 2304
Z_WIDTH = Z_GLR + LANES

N_LEVELS = 7
DIAG_LEVEL = N_LEVELS
FINE_LEVELS = 3


def _rmsnorm(x, gain):
    return x * lax.rsqrt(jnp.mean(x * x, axis=-1, keepdims=True) + EPS) * gain


def _swiglu(nb, wg_ref, wu_ref, wd_ref):
    acc = None
    pending = None
    for lo, hi in FF_CHUNKS:
        g = jnp.dot(nb, wg_ref[:, lo:hi], preferred_element_type=F32)
        u = jnp.dot(nb, wu_ref[:, lo:hi], preferred_element_type=F32)
        if pending is not None:
            a, rows = pending
            d = jnp.dot(a, wd_ref[rows, :], preferred_element_type=F32)
            acc = d if acc is None else acc + d
        pending = ((g * jax.nn.sigmoid(g) * u).astype(BF16), slice(lo, hi))
    return acc, pending


def _swiglu_units(nb, wg_ref, wu_ref, wd_ref, acc_box):
    acc = None
    for lo in range(0, D_FF, MXU_DIM):
        sl = slice(lo, lo + MXU_DIM)
        g = jnp.dot(nb, wg_ref[:, sl], preferred_element_type=F32)
        yield
        u = jnp.dot(nb, wu_ref[:, sl], preferred_element_type=F32)
        yield
        a = (g * jax.nn.sigmoid(g) * u).astype(BF16)
        d = jnp.dot(a, wd_ref[sl, :], preferred_element_type=F32)
        acc = d if acc is None else acc + d
        if lo + MXU_DIM < D_FF:
            yield
    acc_box.append(acc)
    yield


def _ffn1_mixer_kernel(sinks_ref, x_ref, n1_ref, wg_ref, wu_ref, wd_ref, n2_ref, win_ref,
                       cos_ref, sin_ref, qn_ref, kn_ref, wup_ref, bg_ref, gon_ref, tri_ref, mfine_ref, lid_ref,
                       h_ref, mix_ref, n_ref, z_ref, kprev_ref, vprev_ref, state_ref):
    t = pl.program_id(0)
    blk = jnp.maximum(t - 1, 0) % BLOCKS_PER_SEQ

    @pl.when(t == 0)
    def _():
        n_ref[...] = jnp.zeros_like(n_ref)

    @pl.when(blk == 0)
    def _():
        kprev_ref[...] = jnp.zeros_like(kprev_ref)
        vprev_ref[...] = jnp.zeros_like(vprev_ref)
        state_ref[...] = jnp.zeros_like(state_ref)

    z_ref[...] = jnp.dot(n_ref[...], win_ref[...], preferred_element_type=F32)

    def mixer_tile(j):
        rows = pl.ds(j * TILE, TILE)
        prev_ok = (blk > 0) if j == 0 else True
        return _mixer_tile(prev_ok, sinks_ref, z_ref.at[rows], cos_ref.at[rows], sin_ref.at[rows], qn_ref, kn_ref,
                           wup_ref, bg_ref, gon_ref, tri_ref, mfine_ref, lid_ref, mix_ref.at[rows], kprev_ref, vprev_ref,
                           state_ref)

    acc_box = []
    ffn = _swiglu_units(_rmsnorm(x_ref[...], n1_ref[...]).astype(BF16), wg_ref, wu_ref, wd_ref, acc_box)
    tiles = [mixer_tile(j) for j in range(ROW_TILE // TILE)]
    order = sorted((j * TILE_SKEW + s, j) for j in range(len(tiles)) for s in range(MIXER_STAGES))
    n_mixer = len(order)
    n_ffn = 3 * (D_FF // MXU_DIM)
    done = 0
    for i, (_, j) in enumerate(order):
        next(tiles[j])
        while done * n_mixer < (i + 1) * n_ffn:
            next(ffn)
            done += 1
    assert done == n_ffn and next(ffn, None) is None and all(next(tile, None) is None for tile in tiles)
    h = x_ref[...] + 0.5 * acc_box[0]
    h_ref[...] = h
    n_ref[...] = _rmsnorm(h, n2_ref[...]).astype(BF16)


def _outproj_ffn2_ple_kernel(mix_ref, h1_ref, p_ref, wo_ref, n1_ref, wg_ref, wu_ref, wd_ref,
                             gn_ref, wpg_ref, wpp_ref, pn_ref, out_ref):
    h2 = h1_ref[...] + jnp.dot(mix_ref[...], wo_ref[...], preferred_element_type=F32)
    emb = _rmsnorm(jnp.dot(p_ref[...].astype(BF16), wpp_ref[...], preferred_element_type=F32), pn_ref[...])
    acc, (a_last, cols_last) = _swiglu(_rmsnorm(h2, n1_ref[...]).astype(BF16), wg_ref, wu_ref, wd_ref)
    halves = [slice(r, r + ROW_TILE // 2) for r in range(0, ROW_TILE, ROW_TILE // 2)]
    h3 = [h2[r] + 0.5 * (acc[r] + jnp.dot(a_last[r], wd_ref[cols_last, :], preferred_element_type=F32))
          for r in halves]
    for r, h3_r in zip(halves, h3):
        gate = jax.nn.sigmoid(jnp.dot(_rmsnorm(h3_r, gn_ref[...]).astype(BF16), wpg_ref[...],
                                      preferred_element_type=F32))
        out_ref[r, :] = h3_r + gate * emb[r]


def _dot_nt(a, b):
    return lax.dot_general(a, b, (((1,), (1,)), ((), ())), preferred_element_type=F32)


def _dot_tn(a, b):
    return lax.dot_general(a, b, (((0,), (0,)), ((), ())), preferred_element_type=F32)


def _mixer_tile(prev_ok, sinks_ref, z_ref, cos_ref, sin_ref, qn_ref, kn_ref, wup_ref, bg_ref, gon_ref,
                tri_ref, mfine_ref, lid_ref, mix_ref, kprev_ref, vprev_ref, state_ref):
    lane = lax.broadcasted_iota(jnp.int32, (TILE, LANES), 1)
    row = lax.broadcasted_iota(jnp.int32, (TILE, LANES), 0)
    slot0 = lane < ATTN_HEAD_DIM
    first_half = (lane % ATTN_HEAD_DIM) < (ATTN_HEAD_DIM // 2)
    upper = lane > row
    cos = cos_ref[...]
    sin = sin_ref[...]

    def qk_norm_rope(xp, gain):
        sq = xp * xp
        s0 = jnp.sum(jnp.where(slot0, sq, 0.0), axis=-1, keepdims=True)
        s1 = jnp.sum(jnp.where(slot0, 0.0, sq), axis=-1, keepdims=True)
        ms = jnp.where(slot0, s0, s1) * (1.0 / ATTN_HEAD_DIM)
        y = xp * lax.rsqrt(ms + EPS) * gain
        partner = jnp.where(first_half, pltpu.roll(y, LANES - ATTN_HEAD_DIM // 2, 1),
                            pltpu.roll(y, ATTN_HEAD_DIM // 2, 1))
        return y * cos + partner * sin

    kb = qk_norm_rope(z_ref[:, Z_AK:Z_AK + LANES], kn_ref[...]).astype(BF16)
    vb = z_ref[:, Z_AV:Z_AV + LANES].astype(BF16)
    k_band = jnp.concatenate([kprev_ref[...], kb], axis=0)
    v_band = jnp.concatenate([vprev_ref[...], vb], axis=0)
    kprev_ref[...] = kb
    vprev_ref[...] = vb
    upper2 = jnp.concatenate([upper, upper], axis=0)
    row2 = lax.broadcasted_iota(jnp.int32, (2 * TILE, 1), 0)

    def q_rows(p):
        qp = qk_norm_rope(z_ref[:, Z_AQ + p * LANES:Z_AQ + (p + 1) * LANES], qn_ref[...]) * (ATTN_HEAD_DIM ** -0.5)
        return jnp.concatenate([jnp.where(slot0, qp, 0.0), jnp.where(slot0, 0.0, qp)], axis=0).astype(BF16)

    def softmax(p, sc):
        s_prev = sc[:, :TILE] if prev_ok is True else jnp.where(prev_ok, sc[:, :TILE], -jnp.inf)
        sc = jnp.where(upper2, s_prev, sc[:, TILE:])
        sink = jnp.where(row2 < TILE, sinks_ref[p], sinks_ref[p + ATTN_HEADS // 2])
        m = jnp.maximum(jnp.max(sc, axis=-1, keepdims=True), sink)
        e = jnp.exp(sc - m)
        denom = jnp.sum(e, axis=-1, keepdims=True) + jnp.exp(sink - m)
        probs = e * (1.0 / denom)
        return jnp.concatenate([jnp.where(upper2, probs, 0.0), jnp.where(upper2, 0.0, probs)], axis=1).astype(BF16)

    pairs = range(ATTN_HEADS // 2)

    x = jnp.dot(z_ref[:, Z_GLR:Z_GLR + LANES].astype(BF16), wup_ref[...], preferred_element_type=F32) + bg_ref[...]
    g = (jnp.minimum(x, 0.0) - jnp.log1p(jnp.exp(-jnp.abs(x)))) * (1.0 / GLA_GATE_TAU)
    g_hi = g.astype(BF16)
    r1 = g - g_hi.astype(F32)
    g_mid = r1.astype(BF16)
    g_lo = (r1 - g_mid.astype(F32)).astype(BF16)
    g_split2 = jnp.concatenate([g_hi, g_mid], axis=0)
    g_split3 = jnp.concatenate([g_hi, g_mid, g_lo], axis=0)
    yield
    sc_all = _dot_nt(jnp.concatenate([q_rows(p) for p in pairs], axis=0), k_band)
    yield
    b = jnp.dot(tri_ref[...], g_split3, preferred_element_type=F32)
    fine = jnp.exp(jnp.dot(mfine_ref[...], g_split2, preferred_element_type=F32))

    def level_decay(l):
        if l >= N_LEVELS - FINE_LEVELS:
            k = l - (N_LEVELS - FINE_LEVELS)
            return fine[k * TILE:(k + 1) * TILE]
        size = (TILE // 2) >> l
        parts = []
        for m in range(TILE // (2 * size)):
            lo, piv = 2 * m * size, (2 * m + 1) * size - 1
            bp = b[piv:piv + 1]
            parts += [bp - b[lo:piv + 1], b[piv + 1:piv + 1 + size] - bp]
        return jnp.exp(jnp.concatenate(parts, axis=0))

    decays = [level_decay(l) for l in range(N_LEVELS)]
    decay_b = jnp.exp(b)
    decay_s = jnp.exp(b[TILE - 1:TILE] - b)
    ones = jnp.ones((2 * TILE, LANES), BF16)
    tile_decay = jnp.exp(_dot_tn(g_split2, ones))
    yield

    def attention_outputs():
        probs_all = jnp.concatenate([softmax(p, sc_all[2 * p * TILE:2 * (p + 1) * TILE]) for p in pairs], axis=0)
        o_all = jnp.dot(probs_all, v_band, preferred_element_type=F32)
        for p in pairs:
            o2 = o_all[2 * p * TILE:2 * (p + 1) * TILE]
            mix_ref[:, p * LANES:(p + 1) * LANES] = jnp.where(slot0, o2[:TILE], o2[TILE:]).astype(BF16)

    lid2 = lid_ref[...]

    def intra_scores(pp):
        sl = slice(pp * LANES, (pp + 1) * LANES)
        qp = z_ref[:, Z_GQ + pp * LANES:Z_GQ + (pp + 1) * LANES] * (GLA_DK ** -0.5)
        kp = z_ref[:, Z_GK + pp * LANES:Z_GK + (pp + 1) * LANES]
        q2 = jnp.concatenate([jnp.where(slot0, qp, 0.0), jnp.where(slot0, 0.0, qp)], axis=0)
        qk = qp * kp
        diag = jnp.concatenate([jnp.sum(jnp.where(slot0, qk, 0.0), axis=-1, keepdims=True),
                                jnp.sum(jnp.where(slot0, 0.0, qk), axis=-1, keepdims=True)], axis=0)
        attn = jnp.where(lid2 == DIAG_LEVEL, diag, 0.0)
        for l in range(N_LEVELS):
            xl = decays[l][:, sl]
            ql = (q2 * jnp.concatenate([xl, xl], axis=0)).astype(BF16)
            kl = (kp * xl).astype(BF16)
            attn = jnp.where(lid2 == l, _dot_nt(ql, kl), attn)
        xb = decay_b[:, sl]
        xs = decay_s[:, sl]
        qe = (q2 * jnp.concatenate([xb, xb], axis=0)).astype(BF16)
        return attn.astype(BF16), qe, (kp * xs).astype(BF16)

    def outputs(pp, attn_b, qe, ks):
        st = state_ref[pp * LANES:(pp + 1) * LANES, :]
        stb = st.astype(BF16)
        v_wide = z_ref[:, Z_GV + 2 * pp * GLA_DV:Z_GV + 2 * (pp + 1) * GLA_DV].astype(BF16)
        for hh in range(2):
            h = 2 * pp + hh
            lhs = jnp.concatenate([attn_b[hh * TILE:(hh + 1) * TILE], qe[hh * TILE:(hh + 1) * TILE]], axis=1)
            rhs = jnp.concatenate([v_wide[:, hh * GLA_DV:(hh + 1) * GLA_DV], stb], axis=0)
            o = jnp.dot(lhs, rhs, preferred_element_type=F32)
            gog = z_ref[:, Z_GOG + h * GLA_DV:Z_GOG + (h + 1) * GLA_DV]
            o = _rmsnorm(o, gon_ref[...]) * (gog * jax.nn.sigmoid(gog))
            mix_ref[:, ATTN_WIDTH + h * GLA_DV:ATTN_WIDTH + (h + 1) * GLA_DV] = o.astype(BF16)
        upd = _dot_tn(ks, v_wide)
        td = tile_decay[pp * LANES:(pp + 1) * LANES, :]
        half = GLA_DK
        state_ref[pp * LANES:pp * LANES + half, :] = td[:half] * st[:half] + upd[:half, :GLA_DV]
        state_ref[pp * LANES + half:(pp + 1) * LANES, :] = td[half:] * st[half:] + upd[half:, GLA_DV:]

    attention_outputs()
    yield
    intra0 = intra_scores(0)
    yield
    intra1 = intra_scores(1)
    yield
    outputs(0, *intra0)
    yield
    outputs(1, *intra1)
    yield


def _gla_constants():
    r = np.arange(TILE)[:, None]
    t = np.arange(TILE)[None, :]
    mats = []
    for l in range(N_LEVELS - FINE_LEVELS, N_LEVELS):
        size = (TILE // 2) >> l
        same = (r // size) == (t // size)
        odd = ((r // size) % 2) == 1
        mats.append(same & ((odd & (t <= r)) | (~odd & (t > r))))
    mfine = np.concatenate(mats, axis=0).astype(np.float32)
    mfine = np.concatenate([mfine, mfine], axis=1)
    tri = np.concatenate([(t <= r).astype(np.float32)] * 3, axis=1)
    x = r ^ t
    top_bit = np.floor(np.log2(np.maximum(x, 1))).astype(np.int32)
    lid = np.where(t < r, (N_LEVELS - 1) - top_bit, np.where(t == r, DIAG_LEVEL, -1)).astype(np.int32)
    return jnp.asarray(tri, BF16), jnp.asarray(mfine, BF16), jnp.asarray(np.concatenate([lid, lid], axis=0))


def _rope_tables(seq):
    dh = ATTN_HEAD_DIM
    inv_freq = ROPE_THETA ** (-jnp.arange(0, dh, 2, dtype=F32) / dh)
    ang = jnp.arange(seq).astype(F32)[:, None] * inv_freq[None, :]
    cos, sin = jnp.cos(ang), jnp.sin(ang)
    cos_t = jnp.tile(jnp.concatenate([cos, cos], axis=-1), (1, LANES // dh))
    sin_t = jnp.tile(jnp.concatenate([-sin, sin], axis=-1), (1, LANES // dh))
    return cos_t, sin_t


def _prep_w_in(w_in):
    rows = w_in.shape[0]
    mid = ATTN_WIDTH + 2 * ATTN_KV_WIDTH + 2 * GLA_QK_WIDTH + GLA_WIDTH
    aq = w_in[:, :ATTN_WIDTH].astype(BF16).reshape(rows, 2, ATTN_HEADS // 2, ATTN_HEAD_DIM)
    aq = aq.transpose(0, 2, 1, 3).reshape(rows, ATTN_WIDTH)
    glr = jnp.pad(w_in[:, mid:mid + GLA_GATE_RANK].astype(BF16), ((0, 0), (0, LANES - GLA_GATE_RANK)))
    return jnp.concatenate([aq, w_in[:, ATTN_WIDTH:mid].astype(BF16), w_in[:, mid + GLA_GATE_RANK:].astype(BF16),
                            glr], axis=-1)


def _resident(shape):
    return pl.BlockSpec(shape, lambda *_: (0,) * len(shape), pipeline_mode=pl.Buffered(1))


def _layer(layer, h, p, ffn1_norm, ffn1_w_gate, ffn1_w_up, ffn1_w_down, mix_norm, w_in, q_norm, k_norm, attn_sinks,
           gla_w_gate_up, gla_b_gate, gla_out_norm, w_out, ffn2_norm, ffn2_w_gate, ffn2_w_up, ffn2_w_down,
           ple_gate_norm, ple_w_gate, ple_w_proj, ple_norm):
    batch, seq, d = h.shape
    tokens = batch * seq
    x2 = h.reshape(tokens, d)
    rows = lambda width: pl.BlockSpec((ROW_TILE, width), lambda i: (i, 0))
    vec = lambda v: v.reshape(1, -1).astype(F32)

    assert seq == BLOCKS_PER_SEQ * ROW_TILE
    nblk = tokens // ROW_TILE
    cos_t, sin_t = _rope_tables(seq)
    tri, mfine, lid2 = _gla_constants()
    wup = jnp.pad(gla_w_gate_up, ((0, LANES - GLA_GATE_RANK), (0, 0))).astype(BF16)
    ffn_rows = pl.BlockSpec((ROW_TILE, d), lambda t: (jnp.minimum(t, nblk - 1), 0))
    mix_rows = pl.BlockSpec((ROW_TILE, d), lambda t: (jnp.maximum(t - 1, 0), 0))
    rope_rows = pl.BlockSpec((ROW_TILE, LANES), lambda t: (jnp.maximum(t - 1, 0) % BLOCKS_PER_SEQ, 0))
    h1, mix = pl.pallas_call(
        _ffn1_mixer_kernel,
        grid=(nblk + 1,),
        in_specs=[pl.BlockSpec(memory_space=pltpu.SMEM),
                  ffn_rows, _resident((1, d)), _resident((d, D_FF)), _resident((d, D_FF)), _resident((D_FF, d)),
                  _resident((1, d)), _resident((d, Z_WIDTH)),
                  rope_rows, rope_rows, _resident((1, LANES)), _resident((1, LANES)),
                  _resident((LANES, GLA_QK_WIDTH)), _resident((1, GLA_QK_WIDTH)), _resident((1, GLA_DV)),
                  _resident((TILE, 3 * TILE)), _resident((FINE_LEVELS * TILE, 2 * TILE)),
                  _resident((2 * TILE, TILE))],
        out_specs=[ffn_rows, mix_rows],
        out_shape=[jax.ShapeDtypeStruct((tokens, d), F32), jax.ShapeDtypeStruct((tokens, d), BF16)],
        scratch_shapes=[pltpu.VMEM((ROW_TILE, D_MODEL), BF16), pltpu.VMEM((ROW_TILE, Z_WIDTH), F32),
                        pltpu.VMEM((TILE, LANES), BF16), pltpu.VMEM((TILE, LANES), BF16),
                        pltpu.VMEM((GLA_QK_WIDTH, GLA_DV), F32)],
        compiler_params=pltpu.CompilerParams(dimension_semantics=("arbitrary",), vmem_limit_bytes=VMEM_LIMIT,
                                             flags=FUSED_FLAGS),
        name="ffn1_mixer",
    )(attn_sinks.astype(F32), x2, vec(ffn1_norm), ffn1_w_gate.astype(BF16), ffn1_w_up.astype(BF16),
      ffn1_w_down.astype(BF16), vec(mix_norm), _prep_w_in(w_in),
      cos_t, sin_t, vec(jnp.tile(q_norm, 2)), vec(jnp.tile(k_norm, 2)), wup, vec(gla_b_gate), vec(gla_out_norm),
      tri, mfine, lid2)

    wo_attn = w_out[:ATTN_WIDTH].astype(BF16).reshape(2, ATTN_HEADS // 2, ATTN_HEAD_DIM, d)
    wo = jnp.concatenate([wo_attn.transpose(1, 0, 2, 3).reshape(ATTN_WIDTH, d), w_out[ATTN_WIDTH:].astype(BF16)],
                         axis=0)
    out = pl.pallas_call(
        _outproj_ffn2_ple_kernel,
        grid=(tokens // ROW_TILE,),
        in_specs=[rows(d), rows(d), pl.BlockSpec((None, ROW_TILE, PLE_DIM), lambda r: (layer, r, 0)),
                  _resident((d, d)), _resident((1, d)), _resident((d, D_FF)),
                  _resident((d, D_FF)), _resident((D_FF, d)), _resident((1, d)), _resident((d, d)),
                  _resident((PLE_DIM, d)), _resident((1, d))],
        out_specs=rows(d),
        out_shape=jax.ShapeDtypeStruct((tokens, d), F32),
        compiler_params=pltpu.CompilerParams(dimension_semantics=("arbitrary",), vmem_limit_bytes=VMEM_LIMIT),
        name="outproj_ffn2_ple",
    )(mix, h1, p.reshape(p.shape[0], tokens, PLE_DIM), wo, vec(ffn2_norm), ffn2_w_gate.astype(BF16), ffn2_w_up.astype(BF16),
      ffn2_w_down.astype(BF16), vec(ple_gate_norm), ple_w_gate.astype(BF16), ple_w_proj.astype(BF16),
      vec(ple_norm))
    return out.reshape(batch, seq, d)


@jax.jit
def kernel(x, p, ffn1_norm, ffn1_w_gate, ffn1_w_up, ffn1_w_down, mix_norm, w_in, q_norm, k_norm, attn_sinks,
           gla_w_gate_up, gla_b_gate, gla_out_norm, w_out, ffn2_norm, ffn2_w_gate, ffn2_w_up, ffn2_w_down,
           ple_gate_norm, ple_w_gate, ple_w_proj, ple_norm):
    h = x
    for i in range(p.shape[0]):
        h = _layer(i, h, p, ffn1_norm[i], ffn1_w_gate[i], ffn1_w_up[i], ffn1_w_down[i], mix_norm[i], w_in[i],
                   q_norm[i], k_norm[i], attn_sinks[i], gla_w_gate_up[i], gla_b_gate[i], gla_out_norm[i], w_out[i],
                   ffn2_norm[i], ffn2_w_gate[i], ffn2_w_up[i], ffn2_w_down[i], ple_gate_norm[i], ple_w_gate[i],
                   ple_w_proj[i], ple_norm[i])
    return h
```

```python
import numpy as np
import jax
import jax.numpy as jnp
from jax import lax
from jax.experimental import pallas as pl
from jax.experimental.pallas import tpu as pltpu

F32 = jnp.float32
BF16 = jnp.bfloat16

D_MODEL = 1024
D_FF = 2816
PLE_DIM = 256
EPS = 1e-6

ATTN_HEAD_DIM = 64
ATTN_HEADS = 8
ATTN_KV_HEADS = 2
ATTN_WIDTH = ATTN_HEADS * ATTN_HEAD_DIM
ATTN_KV_WIDTH = ATTN_KV_HEADS * ATTN_HEAD_DIM
ROPE_THETA = 10000.0

GLA_HEADS = 4
GLA_DV = 128
GLA_DK = 64
GLA_QK_WIDTH = GLA_HEADS * GLA_DK
GLA_WIDTH = GLA_HEADS * GLA_DV
GLA_GATE_RANK = 16
GLA_GATE_TAU = 16.0

LANES = 128
TILE = 128
ROW_TILE = 512
BLOCKS_PER_SEQ = 8
MXU_DIM = 256
FF_CHUNKS = tuple((lo, min(lo + 3 * MXU_DIM, D_FF)) for lo in range(0, D_FF, 3 * MXU_DIM))
MIXER_STAGES = 8
TILE_SKEW = 5
VMEM_LIMIT = 56 * 1024 * 1024
FUSED_FLAGS = {}

Z_AQ, Z_AK, Z_AV = 0, 512, 640
Z_GQ, Z_GK, Z_GLR, Z_GV, Z_GOG = 768, 1024, 1280, 1408, 1920
Z_WIDTH = Z_GOG + GLA_WIDTH
Z_EARLY = Z_GV
LATE_PROJECTION_AFTER = 2

N_LEVELS = 7
DIAG_LEVEL = N_LEVELS
FINE_LEVELS = 3


def _rmsnorm(x, gain):
    return x * lax.rsqrt(jnp.mean(x * x, axis=-1, keepdims=True) + EPS) * gain


def _swiglu(nb, wg_ref, wu_ref, wd_ref):
    acc = None
    pending = None
    for lo, hi in FF_CHUNKS:
        g = jnp.dot(nb, wg_ref[:, lo:hi], preferred_element_type=F32)
        u = jnp.dot(nb, wu_ref[:, lo:hi], preferred_element_type=F32)
        if pending is not None:
            a, rows = pending
            d = jnp.dot(a, wd_ref[rows, :], preferred_element_type=F32)
            acc = d if acc is None else acc + d
        pending = ((g * jax.nn.sigmoid(g) * u).astype(BF16), slice(lo, hi))
    return acc, pending


def _swiglu_units(nb, wg_ref, wu_ref, wd_ref, acc_box):
    acc = None
    for lo in range(0, D_FF, MXU_DIM):
        sl = slice(lo, lo + MXU_DIM)
        g = jnp.dot(nb, wg_ref[:, sl], preferred_element_type=F32)
        yield
        u = jnp.dot(nb, wu_ref[:, sl], preferred_element_type=F32)
        yield
        a = (g * jax.nn.sigmoid(g) * u).astype(BF16)
        d = jnp.dot(a, wd_ref[sl, :], preferred_element_type=F32)
        acc = d if acc is None else acc + d
        if lo + MXU_DIM < D_FF:
            yield
    acc_box.append(acc)
    yield


def _ffn1_mixer_kernel(sinks_ref, x_ref, n1_ref, wg_ref, wu_ref, wd_ref, n2_ref, win_ref,
                       cos_ref, sin_ref, qn_ref, kn_ref, wup_ref, bg_ref, gon_ref, tri_ref, mfine_ref, lid_ref,
                       h_ref, mix_ref, n_ref, z_ref, kprev_ref, vprev_ref, state_ref):
    t = pl.program_id(0)
    blk = jnp.maximum(t - 1, 0) % BLOCKS_PER_SEQ

    @pl.when(t == 0)
    def _():
        n_ref[...] = jnp.zeros_like(n_ref)

    @pl.when(blk == 0)
    def _():
        kprev_ref[...] = jnp.zeros_like(kprev_ref)
        vprev_ref[...] = jnp.zeros_like(vprev_ref)
        state_ref[...] = jnp.zeros_like(state_ref)

    z_ref[:, :Z_EARLY] = jnp.dot(n_ref[...], win_ref[:, :Z_EARLY], preferred_element_type=F32)

    def mixer_tile(j):
        rows = pl.ds(j * TILE, TILE)
        prev_ok = (blk > 0) if j == 0 else True
        return _mixer_tile(prev_ok, sinks_ref, z_ref.at[rows], cos_ref.at[rows], sin_ref.at[rows], qn_ref, kn_ref,
                           wup_ref, bg_ref, gon_ref, tri_ref, mfine_ref, lid_ref, mix_ref.at[rows], kprev_ref, vprev_ref,
                           state_ref)

    acc_box = []
    ffn = _swiglu_units(_rmsnorm(x_ref[...], n1_ref[...]).astype(BF16), wg_ref, wu_ref, wd_ref, acc_box)
    tiles = [mixer_tile(j) for j in range(ROW_TILE // TILE)]
    order = sorted((j * TILE_SKEW + s, j) for j in range(len(tiles)) for s in range(MIXER_STAGES))
    n_mixer = len(order)
    n_ffn = 3 * (D_FF // MXU_DIM)
    done = 0
    for i, (_, j) in enumerate(order):
        next(tiles[j])
        if i + 1 == LATE_PROJECTION_AFTER:
            z_ref[:, Z_EARLY:] = jnp.dot(n_ref[...], win_ref[:, Z_EARLY:], preferred_element_type=F32)
        while done * n_mixer < (i + 1) * n_ffn:
            next(ffn)
            done += 1
    assert done == n_ffn and next(ffn, None) is None and all(next(tile, None) is None for tile in tiles)
    h = x_ref[...] + 0.5 * acc_box[0]
    h_ref[...] = h
    n_ref[...] = _rmsnorm(h, n2_ref[...]).astype(BF16)


def _outproj_ffn2_ple_kernel(mix_ref, h1_ref, p_ref, wo_ref, n1_ref, wg_ref, wu_ref, wd_ref,
                             gn_ref, wpg_ref, wpp_ref, pn_ref, out_ref):
    h2 = h1_ref[...] + jnp.dot(mix_ref[...], wo_ref[...], preferred_element_type=F32)
    emb = _rmsnorm(jnp.dot(p_ref[...].astype(BF16), wpp_ref[...], preferred_element_type=F32), pn_ref[...])
    acc, (a_last, cols_last) = _swiglu(_rmsnorm(h2, n1_ref[...]).astype(BF16), wg_ref, wu_ref, wd_ref)
    halves = [slice(r, r + ROW_TILE // 2) for r in range(0, ROW_TILE, ROW_TILE // 2)]
    h3 = [h2[r] + 0.5 * (acc[r] + jnp.dot(a_last[r], wd_ref[cols_last, :], preferred_element_type=F32))
          for r in halves]
    for r, h3_r in zip(halves, h3):
        gate = jax.nn.sigmoid(jnp.dot(_rmsnorm(h3_r, gn_ref[...]).astype(BF16), wpg_ref[...],
                                      preferred_element_type=F32))
        out_ref[r, :] = h3_r + gate * emb[r]


def _dot_nt(a, b):
    return lax.dot_general(a, b, (((1,), (1,)), ((), ())), preferred_element_type=F32)


def _dot_tn(a, b):
    return lax.dot_general(a, b, (((0,), (0,)), ((), ())), preferred_element_type=F32)


def _mixer_tile(prev_ok, sinks_ref, z_ref, cos_ref, sin_ref, qn_ref, kn_ref, wup_ref, bg_ref, gon_ref,
                tri_ref, mfine_ref, lid_ref, mix_ref, kprev_ref, vprev_ref, state_ref):
    lane = lax.broadcasted_iota(jnp.int32, (TILE, LANES), 1)
    row = lax.broadcasted_iota(jnp.int32, (TILE, LANES), 0)
    slot0 = lane < ATTN_HEAD_DIM
    first_half = (lane % ATTN_HEAD_DIM) < (ATTN_HEAD_DIM // 2)
    upper = lane > row
    cos = cos_ref[...]
    sin = sin_ref[...]

    def qk_norm_rope(xp, gain):
        sq = xp * xp
        s0 = jnp.sum(jnp.where(slot0, sq, 0.0), axis=-1, keepdims=True)
        s1 = jnp.sum(jnp.where(slot0, 0.0, sq), axis=-1, keepdims=True)
        ms = jnp.where(slot0, s0, s1) * (1.0 / ATTN_HEAD_DIM)
        y = xp * lax.rsqrt(ms + EPS) * gain
        partner = jnp.where(first_half, pltpu.roll(y, LANES - ATTN_HEAD_DIM // 2, 1),
                            pltpu.roll(y, ATTN_HEAD_DIM // 2, 1))
        return y * cos + partner * sin

    kb = qk_norm_rope(z_ref[:, Z_AK:Z_AK + LANES], kn_ref[...]).astype(BF16)
    vb = z_ref[:, Z_AV:Z_AV + LANES].astype(BF16)
    k_band = jnp.concatenate([kprev_ref[...], kb], axis=0)
    v_band = jnp.concatenate([vprev_ref[...], vb], axis=0)
    kprev_ref[...] = kb
    vprev_ref[...] = vb
    upper2 = jnp.concatenate([upper, upper], axis=0)
    row2 = lax.broadcasted_iota(jnp.int32, (2 * TILE, 1), 0)

    def q_rows(p):
        qp = qk_norm_rope(z_ref[:, Z_AQ + p * LANES:Z_AQ + (p + 1) * LANES], qn_ref[...]) * (ATTN_HEAD_DIM ** -0.5)
        return jnp.concatenate([jnp.where(slot0, qp, 0.0), jnp.where(slot0, 0.0, qp)], axis=0).astype(BF16)

    def softmax(p, sc):
        s_prev = sc[:, :TILE] if prev_ok is True else jnp.where(prev_ok, sc[:, :TILE], -jnp.inf)
        sc = jnp.where(upper2, s_prev, sc[:, TILE:])
        sink = jnp.where(row2 < TILE, sinks_ref[p], sinks_ref[p + ATTN_HEADS // 2])
        m = jnp.maximum(jnp.max(sc, axis=-1, keepdims=True), sink)
        e = jnp.exp(sc - m)
        denom = jnp.sum(e, axis=-1, keepdims=True) + jnp.exp(sink - m)
        probs = e * (1.0 / denom)
        return jnp.concatenate([jnp.where(upper2, probs, 0.0), jnp.where(upper2, 0.0, probs)], axis=1).astype(BF16)

    pairs = range(ATTN_HEADS // 2)

    x = jnp.dot(z_ref[:, Z_GLR:Z_GLR + LANES].astype(BF16), wup_ref[...], preferred_element_type=F32) + bg_ref[...]
    g = (jnp.minimum(x, 0.0) - jnp.log1p(jnp.exp(-jnp.abs(x)))) * (1.0 / GLA_GATE_TAU)
    g_hi = g.astype(BF16)
    r1 = g - g_hi.astype(F32)
    g_mid = r1.astype(BF16)
    g_lo = (r1 - g_mid.astype(F32)).astype(BF16)
    g_split2 = jnp.concatenate([g_hi, g_mid], axis=0)
    g_split3 = jnp.concatenate([g_hi, g_mid, g_lo], axis=0)
    yield
    sc_all = _dot_nt(jnp.concatenate([q_rows(p) for p in pairs], axis=0), k_band)
    yield
    b = jnp.dot(tri_ref[...], g_split3, preferred_element_type=F32)
    fine = jnp.exp(jnp.dot(mfine_ref[...], g_split2, preferred_element_type=F32))

    def level_decay(l):
        if l >= N_LEVELS - FINE_LEVELS:
            k = l - (N_LEVELS - FINE_LEVELS)
            return fine[k * TILE:(k + 1) * TILE]
        size = (TILE // 2) >> l
        parts = []
        for m in range(TILE // (2 * size)):
            lo, piv = 2 * m * size, (2 * m + 1) * size - 1
            bp = b[piv:piv + 1]
            parts += [bp - b[lo:piv + 1], b[piv + 1:piv + 1 + size] - bp]
        return jnp.exp(jnp.concatenate(parts, axis=0))

    decays = [level_decay(l) for l in range(N_LEVELS)]
    decay_b = jnp.exp(b)
    decay_s = jnp.exp(b[TILE - 1:TILE] - b)
    ones = jnp.ones((2 * TILE, LANES), BF16)
    tile_decay = jnp.exp(_dot_tn(g_split2, ones))
    yield

    def attention_outputs():
        probs_all = jnp.concatenate([softmax(p, sc_all[2 * p * TILE:2 * (p + 1) * TILE]) for p in pairs], axis=0)
        o_all = jnp.dot(probs_all, v_band, preferred_element_type=F32)
        for p in pairs:
            o2 = o_all[2 * p * TILE:2 * (p + 1) * TILE]
            mix_ref[:, p * LANES:(p + 1) * LANES] = jnp.where(slot0, o2[:TILE], o2[TILE:]).astype(BF16)

    lid2 = lid_ref[...]

    def intra_scores(pp):
        sl = slice(pp * LANES, (pp + 1) * LANES)
        qp = z_ref[:, Z_GQ + pp * LANES:Z_GQ + (pp + 1) * LANES] * (GLA_DK ** -0.5)
        kp = z_ref[:, Z_GK + pp * LANES:Z_GK + (pp + 1) * LANES]
        q2 = jnp.concatenate([jnp.where(slot0, qp, 0.0), jnp.where(slot0, 0.0, qp)], axis=0)
        qk = qp * kp
        diag = jnp.concatenate([jnp.sum(jnp.where(slot0, qk, 0.0), axis=-1, keepdims=True),
                                jnp.sum(jnp.where(slot0, 0.0, qk), axis=-1, keepdims=True)], axis=0)
        attn = jnp.where(lid2 == DIAG_LEVEL, diag, 0.0)
        for l in range(N_LEVELS):
            xl = decays[l][:, sl]
            ql = (q2 * jnp.concatenate([xl, xl], axis=0)).astype(BF16)
            kl = (kp * xl).astype(BF16)
            attn = jnp.where(lid2 == l, _dot_nt(ql, kl), attn)
        xb = decay_b[:, sl]
        xs = decay_s[:, sl]
        qe = (q2 * jnp.concatenate([xb, xb], axis=0)).astype(BF16)
        return attn.astype(BF16), qe, (kp * xs).astype(BF16)

    def outputs(pp, attn_b, qe, ks):
        st = state_ref[pp * LANES:(pp + 1) * LANES, :]
        stb = st.astype(BF16)
        v_wide = z_ref[:, Z_GV + 2 * pp * GLA_DV:Z_GV + 2 * (pp + 1) * GLA_DV].astype(BF16)
        for hh in range(2):
            h = 2 * pp + hh
            lhs = jnp.concatenate([attn_b[hh * TILE:(hh + 1) * TILE], qe[hh * TILE:(hh + 1) * TILE]], axis=1)
            rhs = jnp.concatenate([v_wide[:, hh * GLA_DV:(hh + 1) * GLA_DV], stb], axis=0)
            o = jnp.dot(lhs, rhs, preferred_element_type=F32)
            gog = z_ref[:, Z_GOG + h * GLA_DV:Z_GOG + (h + 1) * GLA_DV]
            o = _rmsnorm(o, gon_ref[...]) * (gog * jax.nn.sigmoid(gog))
            mix_ref[:, ATTN_WIDTH + h * GLA_DV:ATTN_WIDTH + (h + 1) * GLA_DV] = o.astype(BF16)
        upd = _dot_tn(ks, v_wide)
        td = tile_decay[pp * LANES:(pp + 1) * LANES, :]
        half = GLA_DK
        state_ref[pp * LANES:pp * LANES + half, :] = td[:half] * st[:half] + upd[:half, :GLA_DV]
        state_ref[pp * LANES + half:(pp + 1) * LANES, :] = td[half:] * st[half:] + upd[half:, GLA_DV:]

    attention_outputs()
    yield
    intra0 = intra_scores(0)
    yield
    intra1 = intra_scores(1)
    yield
    outputs(0, *intra0)
    yield
    outputs(1, *intra1)
    yield


def _gla_constants():
    r = np.arange(TILE)[:, None]
    t = np.arange(TILE)[None, :]
    mats = []
    for l in range(N_LEVELS - FINE_LEVELS, N_LEVELS):
        size = (TILE // 2) >> l
        same = (r // size) == (t // size)
        odd = ((r // size) % 2) == 1
        mats.append(same & ((odd & (t <= r)) | (~odd & (t > r))))
    mfine = np.concatenate(mats, axis=0).astype(np.float32)
    mfine = np.concatenate([mfine, mfine], axis=1)
    tri = np.concatenate([(t <= r).astype(np.float32)] * 3, axis=1)
    x = r ^ t
    top_bit = np.floor(np.log2(np.maximum(x, 1))).astype(np.int32)
    lid = np.where(t < r, (N_LEVELS - 1) - top_bit, np.where(t == r, DIAG_LEVEL, -1)).astype(np.int32)
    return jnp.asarray(tri, BF16), jnp.asarray(mfine, BF16), jnp.asarray(np.concatenate([lid, lid], axis=0))


def _rope_tables(seq):
    dh = ATTN_HEAD_DIM
    inv_freq = ROPE_THETA ** (-jnp.arange(0, dh, 2, dtype=F32) / dh)
    ang = jnp.arange(seq).astype(F32)[:, None] * inv_freq[None, :]
    cos, sin = jnp.cos(ang), jnp.sin(ang)
    cos_t = jnp.tile(jnp.concatenate([cos, cos], axis=-1), (1, LANES // dh))
    sin_t = jnp.tile(jnp.concatenate([-sin, sin], axis=-1), (1, LANES // dh))
    return cos_t, sin_t


def _prep_w_in(w_in):
    rows = w_in.shape[0]
    gv0 = ATTN_WIDTH + 2 * ATTN_KV_WIDTH + 2 * GLA_QK_WIDTH
    glr0 = gv0 + GLA_WIDTH
    aq = w_in[:, :ATTN_WIDTH].astype(BF16).reshape(rows, 2, ATTN_HEADS // 2, ATTN_HEAD_DIM)
    aq = aq.transpose(0, 2, 1, 3).reshape(rows, ATTN_WIDTH)
    glr = jnp.pad(w_in[:, glr0:glr0 + GLA_GATE_RANK].astype(BF16), ((0, 0), (0, LANES - GLA_GATE_RANK)))
    return jnp.concatenate([aq, w_in[:, ATTN_WIDTH:gv0].astype(BF16), glr, w_in[:, gv0:glr0].astype(BF16),
                            w_in[:, glr0 + GLA_GATE_RANK:].astype(BF16)], axis=-1)


def _resident(shape):
    return pl.BlockSpec(shape, lambda *_: (0,) * len(shape), pipeline_mode=pl.Buffered(1))


def _layer(layer, h, p, ffn1_norm, ffn1_w_gate, ffn1_w_up, ffn1_w_down, mix_norm, w_in, q_norm, k_norm, attn_sinks,
           gla_w_gate_up, gla_b_gate, gla_out_norm, w_out, ffn2_norm, ffn2_w_gate, ffn2_w_up, ffn2_w_down,
           ple_gate_norm, ple_w_gate, ple_w_proj, ple_norm):
    batch, seq, d = h.shape
    tokens = batch * seq
    x2 = h.reshape(tokens, d)
    rows = lambda width: pl.BlockSpec((ROW_TILE, width), lambda i: (i, 0))
    vec = lambda v: v.reshape(1, -1).astype(F32)

    assert seq == BLOCKS_PER_SEQ * ROW_TILE
    nblk = tokens // ROW_TILE
    cos_t, sin_t = _rope_tables(seq)
    tri, mfine, lid2 = _gla_constants()
    wup = jnp.pad(gla_w_gate_up, ((0, LANES - GLA_GATE_RANK), (0, 0))).astype(BF16)
    ffn_rows = pl.BlockSpec((ROW_TILE, d), lambda t: (jnp.minimum(t, nblk - 1), 0))
    mix_rows = pl.BlockSpec((ROW_TILE, d), lambda t: (jnp.maximum(t - 1, 0), 0))
    rope_rows = pl.BlockSpec((ROW_TILE, LANES), lambda t: (jnp.maximum(t - 1, 0) % BLOCKS_PER_SEQ, 0))
    h1, mix = pl.pallas_call(
        _ffn1_mixer_kernel,
        grid=(nblk + 1,),
        in_specs=[pl.BlockSpec(memory_space=pltpu.SMEM),
                  ffn_rows, _resident((1, d)), _resident((d, D_FF)), _resident((d, D_FF)), _resident((D_FF, d)),
                  _resident((1, d)), _resident((d, Z_WIDTH)),
                  rope_rows, rope_rows, _resident((1, LANES)), _resident((1, LANES)),
                  _resident((LANES, GLA_QK_WIDTH)), _resident((1, GLA_QK_WIDTH)), _resident((1, GLA_DV)),
                  _resident((TILE, 3 * TILE)), _resident((FINE_LEVELS * TILE, 2 * TILE)),
                  _resident((2 * TILE, TILE))],
        out_specs=[ffn_rows, mix_rows],
        out_shape=[jax.ShapeDtypeStruct((tokens, d), F32), jax.ShapeDtypeStruct((tokens, d), BF16)],
        scratch_shapes=[pltpu.VMEM((ROW_TILE, D_MODEL), BF16), pltpu.VMEM((ROW_TILE, Z_WIDTH), F32),
                        pltpu.VMEM((TILE, LANES), BF16), pltpu.VMEM((TILE, LANES), BF16),
                        pltpu.VMEM((GLA_QK_WIDTH, GLA_DV), F32)],
        compiler_params=pltpu.CompilerParams(dimension_semantics=("arbitrary",), vmem_limit_bytes=VMEM_LIMIT,
                                             flags=FUSED_FLAGS),
        name="ffn1_mixer",
    )(attn_sinks.astype(F32), x2, vec(ffn1_norm), ffn1_w_gate.astype(BF16), ffn1_w_up.astype(BF16),
      ffn1_w_down.astype(BF16), vec(mix_norm), _prep_w_in(w_in),
      cos_t, sin_t, vec(jnp.tile(q_norm, 2)), vec(jnp.tile(k_norm, 2)), wup, vec(gla_b_gate), vec(gla_out_norm),
      tri, mfine, lid2)

    wo_attn = w_out[:ATTN_WIDTH].astype(BF16).reshape(2, ATTN_HEADS // 2, ATTN_HEAD_DIM, d)
    wo = jnp.concatenate([wo_attn.transpose(1, 0, 2, 3).reshape(ATTN_WIDTH, d), w_out[ATTN_WIDTH:].astype(BF16)],
                         axis=0)
    out = pl.pallas_call(
        _outproj_ffn2_ple_kernel,
        grid=(tokens // ROW_TILE,),
        in_specs=[rows(d), rows(d), pl.BlockSpec((None, ROW_TILE, PLE_DIM), lambda r: (layer, r, 0)),
                  _resident((d, d)), _resident((1, d)), _resident((d, D_FF)),
                  _resident((d, D_FF)), _resident((D_FF, d)), _resident((1, d)), _resident((d, d)),
                  _resident((PLE_DIM, d)), _resident((1, d))],
        out_specs=rows(d),
        out_shape=jax.ShapeDtypeStruct((tokens, d), F32),
        compiler_params=pltpu.CompilerParams(dimension_semantics=("arbitrary",), vmem_limit_bytes=VMEM_LIMIT),
        name="outproj_ffn2_ple",
    )(mix, h1, p.reshape(p.shape[0], tokens, PLE_DIM), wo, vec(ffn2_norm), ffn2_w_gate.astype(BF16), ffn2_w_up.astype(BF16),
      ffn2_w_down.astype(BF16), vec(ple_gate_norm), ple_w_gate.astype(BF16), ple_w_proj.astype(BF16),
      vec(ple_norm))
    return out.reshape(batch, seq, d)


@jax.jit
def kernel(x, p, ffn1_norm, ffn1_w_gate, ffn1_w_up, ffn1_w_down, mix_norm, w_in, q_norm, k_norm, attn_sinks,
           gla_w_gate_up, gla_b_gate, gla_out_norm, w_out, ffn2_norm, ffn2_w_gate, ffn2_w_up, ffn2_w_down,
           ple_gate_norm, ple_w_gate, ple_w_proj, ple_norm):
    h = x
    for i in range(p.shape[0]):
        h = _layer(i, h, p, ffn1_norm[i], ffn1_w_gate[i], ffn1_w_up[i], ffn1_w_down[i], mix_norm[i], w_in[i],
                   q_norm[i], k_norm[i], attn_sinks[i], gla_w_gate_up[i], gla_b_gate[i], gla_out_norm[i], w_out[i],
                   ffn2_norm[i], ffn2_w_gate[i], ffn2_w_up[i], ffn2_w_down[i], ple_gate_norm[i], ple_w_gate[i],
                   ple_w_proj[i], ple_norm[i])
    return h
```

```python
import numpy as np
import jax
import jax.numpy as jnp
from jax import lax
from jax.experimental import pallas as pl
from jax.experimental.pallas import tpu as pltpu

F32 = jnp.float32
BF16 = jnp.bfloat16

D_MODEL = 1024
D_FF = 2816
PLE_DIM = 256
EPS = 1e-6

ATTN_HEAD_DIM = 64
ATTN_HEADS = 8
ATTN_KV_HEADS = 2
ATTN_WIDTH = ATTN_HEADS * ATTN_HEAD_DIM
ATTN_KV_WIDTH = ATTN_KV_HEADS * ATTN_HEAD_DIM
ROPE_THETA = 10000.0

GLA_HEADS = 4
GLA_DV = 128
GLA_DK = 64
GLA_QK_WIDTH = GLA_HEADS * GLA_DK
GLA_WIDTH = GLA_HEADS * GLA_DV
GLA_GATE_RANK = 16
GLA_GATE_TAU = 16.0

LANES = 128
TILE = 128
ROW_TILE = 256
BLOCKS_PER_SEQ = 16
MXU_DIM = 256
FF_CHUNKS = tuple((lo, min(lo + 3 * MXU_DIM, D_FF)) for lo in range(0, D_FF, 3 * MXU_DIM))
MIXER_STAGES = 8
TILE_SKEW = 3
VMEM_LIMIT = 56 * 1024 * 1024

Z_AQ, Z_AK, Z_AV = 0, 512, 640
Z_GQ, Z_GK, Z_GLR, Z_GV, Z_GOG = 768, 1024, 1280, 1408, 1920
Z_WIDTH = Z_GOG + GLA_WIDTH
Z_EARLY = Z_GV
LATE_PROJECTION_AFTER = 2

N_LEVELS = 7
DIAG_LEVEL = N_LEVELS
FINE_LEVELS = 3


def _rmsnorm(x, gain):
    return x * lax.rsqrt(jnp.mean(x * x, axis=-1, keepdims=True) + EPS) * gain


def _swiglu(nb, wg_ref, wu_ref, wd_ref):
    acc = None
    pending = None
    for lo, hi in FF_CHUNKS:
        g = jnp.dot(nb, wg_ref[:, lo:hi], preferred_element_type=F32)
        u = jnp.dot(nb, wu_ref[:, lo:hi], preferred_element_type=F32)
        if pending is not None:
            a, rows = pending
            d = jnp.dot(a, wd_ref[rows, :], preferred_element_type=F32)
            acc = d if acc is None else acc + d
        pending = ((g * jax.nn.sigmoid(g) * u).astype(BF16), slice(lo, hi))
    return acc, pending


def _swiglu_units(nb, wg_ref, wu_ref, wd_ref, acc_box):
    acc = None
    for lo in range(0, D_FF, MXU_DIM):
        sl = slice(lo, lo + MXU_DIM)
        g = jnp.dot(nb, wg_ref[:, sl], preferred_element_type=F32)
        yield
        u = jnp.dot(nb, wu_ref[:, sl], preferred_element_type=F32)
        yield
        a = (g * jax.nn.sigmoid(g) * u).astype(BF16)
        d = jnp.dot(a, wd_ref[sl, :], preferred_element_type=F32)
        acc = d if acc is None else acc + d
        if lo + MXU_DIM < D_FF:
            yield
    acc_box.append(acc)
    yield


def _ffn1_mixer_kernel(sinks_ref, x_ref, n1_ref, wg_ref, wu_ref, wd_ref, n2_ref, win_ref,
                       cos_ref, sin_ref, qn_ref, kn_ref, wup_ref, bg_ref, gon_ref, tri_ref, mfine_ref, lid_ref,
                       h_ref, mix_ref, n_ref, z_ref, kprev_ref, vprev_ref, state_ref):
    t = pl.program_id(0)
    blk = jnp.maximum(t - 1, 0) % BLOCKS_PER_SEQ

    @pl.when(t == 0)
    def _():
        n_ref[...] = jnp.zeros_like(n_ref)

    @pl.when(blk == 0)
    def _():
        kprev_ref[...] = jnp.zeros_like(kprev_ref)
        vprev_ref[...] = jnp.zeros_like(vprev_ref)
        state_ref[...] = jnp.zeros_like(state_ref)

    z_ref[:, :Z_EARLY] = jnp.dot(n_ref[...], win_ref[:, :Z_EARLY], preferred_element_type=F32)

    def mixer_tile(j):
        rows = pl.ds(j * TILE, TILE)
        prev_ok = (blk > 0) if j == 0 else True
        return _mixer_tile(prev_ok, sinks_ref, z_ref.at[rows], cos_ref.at[rows], sin_ref.at[rows], qn_ref, kn_ref,
                           wup_ref, bg_ref, gon_ref, tri_ref, mfine_ref, lid_ref, mix_ref.at[rows], kprev_ref,
                           vprev_ref, state_ref)

    acc_box = []
    ffn = _swiglu_units(_rmsnorm(x_ref[...], n1_ref[...]).astype(BF16), wg_ref, wu_ref, wd_ref, acc_box)
    tiles = [mixer_tile(j) for j in range(ROW_TILE // TILE)]
    order = sorted((j * TILE_SKEW + s, j) for j in range(len(tiles)) for s in range(MIXER_STAGES))
    n_mixer = len(order)
    n_ffn = 3 * (D_FF // MXU_DIM)
    done = 0
    for i, (_, j) in enumerate(order):
        next(tiles[j])
        if i + 1 == LATE_PROJECTION_AFTER:
            z_ref[:, Z_EARLY:] = jnp.dot(n_ref[...], win_ref[:, Z_EARLY:], preferred_element_type=F32)
        while done * n_mixer < (i + 1) * n_ffn:
            next(ffn)
            done += 1
    assert done == n_ffn and next(ffn, None) is None and all(next(tile, None) is None for tile in tiles)
    h = x_ref[...] + 0.5 * acc_box[0]
    h_ref[...] = h
    n_ref[...] = _rmsnorm(h, n2_ref[...]).astype(BF16)


def _outproj_ffn2_ple_kernel(mix_ref, h1_ref, p_ref, wo_ref, n1_ref, wg_ref, wu_ref, wd_ref,
                             gn_ref, wpg_ref, wpp_ref, pn_ref, out_ref):
    halves = [slice(r, r + ROW_TILE // 2) for r in range(0, ROW_TILE, ROW_TILE // 2)]
    h2 = [h1_ref[r, :] + jnp.dot(mix_ref[r, :], wo_ref[...], preferred_element_type=F32) for r in halves]
    emb = _rmsnorm(jnp.dot(p_ref[...].astype(BF16), wpp_ref[...], preferred_element_type=F32), pn_ref[...])
    nb = jnp.concatenate([_rmsnorm(h2_r, n1_ref[...]).astype(BF16) for h2_r in h2], axis=0)
    acc, (a_last, cols_last) = _swiglu(nb, wg_ref, wu_ref, wd_ref)
    h3 = [h2_r + 0.5 * (acc[r] + jnp.dot(a_last[r], wd_ref[cols_last, :], preferred_element_type=F32))
          for r, h2_r in zip(halves, h2)]
    for r, h3_r in zip(halves, h3):
        gate = jax.nn.sigmoid(jnp.dot(_rmsnorm(h3_r, gn_ref[...]).astype(BF16), wpg_ref[...],
                                      preferred_element_type=F32))
        out_ref[r, :] = h3_r + gate * emb[r]


def _dot_nt(a, b):
    return lax.dot_general(a, b, (((1,), (1,)), ((), ())), preferred_element_type=F32)


def _dot_tn(a, b):
    return lax.dot_general(a, b, (((0,), (0,)), ((), ())), preferred_element_type=F32)


def _mixer_tile(prev_ok, sinks_ref, z_ref, cos_ref, sin_ref, qn_ref, kn_ref, wup_ref, bg_ref, gon_ref,
                tri_ref, mfine_ref, lid_ref, mix_ref, kprev_ref, vprev_ref, state_ref):
    lane = lax.broadcasted_iota(jnp.int32, (TILE, LANES), 1)
    row = lax.broadcasted_iota(jnp.int32, (TILE, LANES), 0)
    slot0 = lane < ATTN_HEAD_DIM
    first_half = (lane % ATTN_HEAD_DIM) < (ATTN_HEAD_DIM // 2)
    upper = lane > row
    cos = cos_ref[...]
    sin = sin_ref[...]

    def qk_norm_rope(xp, gain):
        sq = xp * xp
        s0 = jnp.sum(jnp.where(slot0, sq, 0.0), axis=-1, keepdims=True)
        s1 = jnp.sum(jnp.where(slot0, 0.0, sq), axis=-1, keepdims=True)
        ms = jnp.where(slot0, s0, s1) * (1.0 / ATTN_HEAD_DIM)
        y = xp * lax.rsqrt(ms + EPS) * gain
        partner = jnp.where(first_half, pltpu.roll(y, LANES - ATTN_HEAD_DIM // 2, 1),
                            pltpu.roll(y, ATTN_HEAD_DIM // 2, 1))
        return y * cos + partner * sin

    kb = qk_norm_rope(z_ref[:, Z_AK:Z_AK + LANES], kn_ref[...]).astype(BF16)
    vb = z_ref[:, Z_AV:Z_AV + LANES].astype(BF16)
    k_band = jnp.concatenate([kprev_ref[...], kb], axis=0)
    v_band = jnp.concatenate([vprev_ref[...], vb], axis=0)
    kprev_ref[...] = kb
    vprev_ref[...] = vb
    upper2 = jnp.concatenate([upper, upper], axis=0)
    row2 = lax.broadcasted_iota(jnp.int32, (2 * TILE, 1), 0)

    def q_rows(p):
        qp = qk_norm_rope(z_ref[:, Z_AQ + p * LANES:Z_AQ + (p + 1) * LANES], qn_ref[...]) * (ATTN_HEAD_DIM ** -0.5)
        return jnp.concatenate([jnp.where(slot0, qp, 0.0), jnp.where(slot0, 0.0, qp)], axis=0).astype(BF16)

    def softmax(p, sc):
        s_prev = sc[:, :TILE] if prev_ok is True else jnp.where(prev_ok, sc[:, :TILE], -jnp.inf)
        sc = jnp.where(upper2, s_prev, sc[:, TILE:])
        sink = jnp.where(row2 < TILE, sinks_ref[p], sinks_ref[p + ATTN_HEADS // 2])
        m = jnp.maximum(jnp.max(sc, axis=-1, keepdims=True), sink)
        e = jnp.exp(sc - m)
        denom = jnp.sum(e, axis=-1, keepdims=True) + jnp.exp(sink - m)
        probs = e * (1.0 / denom)
        return jnp.concatenate([jnp.where(upper2, probs, 0.0), jnp.where(upper2, 0.0, probs)], axis=1).astype(BF16)

    pairs = range(ATTN_HEADS // 2)

    x = jnp.dot(z_ref[:, Z_GLR:Z_GLR + LANES].astype(BF16), wup_ref[...], preferred_element_type=F32) + bg_ref[...]
    g = (jnp.minimum(x, 0.0) - jnp.log1p(jnp.exp(-jnp.abs(x)))) * (1.0 / GLA_GATE_TAU)
    g_hi = g.astype(BF16)
    r1 = g - g_hi.astype(F32)
    g_mid = r1.astype(BF16)
    g_lo = (r1 - g_mid.astype(F32)).astype(BF16)
    g_split2 = jnp.concatenate([g_hi, g_mid], axis=0)
    g_split3 = jnp.concatenate([g_hi, g_mid, g_lo], axis=0)
    yield
    sc_all = _dot_nt(jnp.concatenate([q_rows(p) for p in pairs], axis=0), k_band)
    yield
    b = jnp.dot(tri_ref[...], g_split3, preferred_element_type=F32)
    fine = jnp.exp(jnp.dot(mfine_ref[...], g_split2, preferred_element_type=F32))

    def level_decay(l):
        if l >= N_LEVELS - FINE_LEVELS:
            k = l - (N_LEVELS - FINE_LEVELS)
            return fine[k * TILE:(k + 1) * TILE]
        size = (TILE // 2) >> l
        parts = []
        for m in range(TILE // (2 * size)):
            lo, piv = 2 * m * size, (2 * m + 1) * size - 1
            bp = b[piv:piv + 1]
            parts += [bp - b[lo:piv + 1], b[piv + 1:piv + 1 + size] - bp]
        return jnp.exp(jnp.concatenate(parts, axis=0))

    decays = [level_decay(l) for l in range(N_LEVELS)]
    decay_b = jnp.exp(b)
    decay_s = jnp.exp(b[TILE - 1:TILE] - b)
    ones = jnp.ones((2 * TILE, LANES), BF16)
    tile_decay = jnp.exp(_dot_tn(g_split2, ones))
    yield

    def attention_outputs():
        probs_all = jnp.concatenate([softmax(p, sc_all[2 * p * TILE:2 * (p + 1) * TILE]) for p in pairs], axis=0)
        o_all = jnp.dot(probs_all, v_band, preferred_element_type=F32)
        for p in pairs:
            o2 = o_all[2 * p * TILE:2 * (p + 1) * TILE]
            mix_ref[:, p * LANES:(p + 1) * LANES] = jnp.where(slot0, o2[:TILE], o2[TILE:]).astype(BF16)

    lid2 = lid_ref[...]

    def intra_scores(pp):
        sl = slice(pp * LANES, (pp + 1) * LANES)
        qp = z_ref[:, Z_GQ + pp * LANES:Z_GQ + (pp + 1) * LANES] * (GLA_DK ** -0.5)
        kp = z_ref[:, Z_GK + pp * LANES:Z_GK + (pp + 1) * LANES]
        q2 = jnp.concatenate([jnp.where(slot0, qp, 0.0), jnp.where(slot0, 0.0, qp)], axis=0)
        qk = qp * kp
        diag = jnp.concatenate([jnp.sum(jnp.where(slot0, qk, 0.0), axis=-1, keepdims=True),
                                jnp.sum(jnp.where(slot0, 0.0, qk), axis=-1, keepdims=True)], axis=0)
        attn = jnp.where(lid2 == DIAG_LEVEL, diag, 0.0)
        for l in range(N_LEVELS):
            xl = decays[l][:, sl]
            ql = (q2 * jnp.concatenate([xl, xl], axis=0)).astype(BF16)
            kl = (kp * xl).astype(BF16)
            attn = jnp.where(lid2 == l, _dot_nt(ql, kl), attn)
        xb = decay_b[:, sl]
        xs = decay_s[:, sl]
        qe = (q2 * jnp.concatenate([xb, xb], axis=0)).astype(BF16)
        return attn.astype(BF16), qe, (kp * xs).astype(BF16)

    def outputs(pp, attn_b, qe, ks):
        st = state_ref[pp * LANES:(pp + 1) * LANES, :]
        stb = st.astype(BF16)
        v_wide = z_ref[:, Z_GV + 2 * pp * GLA_DV:Z_GV + 2 * (pp + 1) * GLA_DV].astype(BF16)
        for hh in range(2):
            h = 2 * pp + hh
            lhs = jnp.concatenate([attn_b[hh * TILE:(hh + 1) * TILE], qe[hh * TILE:(hh + 1) * TILE]], axis=1)
            rhs = jnp.concatenate([v_wide[:, hh * GLA_DV:(hh + 1) * GLA_DV], stb], axis=0)
            o = jnp.dot(lhs, rhs, preferred_element_type=F32)
            gog = z_ref[:, Z_GOG + h * GLA_DV:Z_GOG + (h + 1) * GLA_DV]
            o = _rmsnorm(o, gon_ref[...]) * (gog * jax.nn.sigmoid(gog))
            mix_ref[:, ATTN_WIDTH + h * GLA_DV:ATTN_WIDTH + (h + 1) * GLA_DV] = o.astype(BF16)
        upd = _dot_tn(ks, v_wide)
        td = tile_decay[pp * LANES:(pp + 1) * LANES, :]
        half = GLA_DK
        state_ref[pp * LANES:pp * LANES + half, :] = td[:half] * st[:half] + upd[:half, :GLA_DV]
        state_ref[pp * LANES + half:(pp + 1) * LANES, :] = td[half:] * st[half:] + upd[half:, GLA_DV:]

    attention_outputs()
    yield
    intra0 = intra_scores(0)
    yield
    intra1 = intra_scores(1)
    yield
    outputs(0, *intra0)
    yield
    outputs(1, *intra1)
    yield


def _gla_constants():
    r = np.arange(TILE)[:, None]
    t = np.arange(TILE)[None, :]
    mats = []
    for l in range(N_LEVELS - FINE_LEVELS, N_LEVELS):
        size = (TILE // 2) >> l
        same = (r // size) == (t // size)
        odd = ((r // size) % 2) == 1
        mats.append(same & ((odd & (t <= r)) | (~odd & (t > r))))
    mfine = np.concatenate(mats, axis=0).astype(np.float32)
    mfine = np.concatenate([mfine, mfine], axis=1)
    tri = np.concatenate([(t <= r).astype(np.float32)] * 3, axis=1)
    x = r ^ t
    top_bit = np.floor(np.log2(np.maximum(x, 1))).astype(np.int32)
    lid = np.where(t < r, (N_LEVELS - 1) - top_bit, np.where(t == r, DIAG_LEVEL, -1)).astype(np.int32)
    return jnp.asarray(tri, BF16), jnp.asarray(mfine, BF16), jnp.asarray(np.concatenate([lid, lid], axis=0))


def _rope_tables(seq):
    dh = ATTN_HEAD_DIM
    inv_freq = ROPE_THETA ** (-jnp.arange(0, dh, 2, dtype=F32) / dh)
    ang = jnp.arange(seq).astype(F32)[:, None] * inv_freq[None, :]
    cos, sin = jnp.cos(ang), jnp.sin(ang)
    cos_t = jnp.tile(jnp.concatenate([cos, cos], axis=-1), (1, LANES // dh))
    sin_t = jnp.tile(jnp.concatenate([-sin, sin], axis=-1), (1, LANES // dh))
    return cos_t, sin_t


def _prep_w_in(w_in):
    rows = w_in.shape[0]
    gv0 = ATTN_WIDTH + 2 * ATTN_KV_WIDTH + 2 * GLA_QK_WIDTH
    glr0 = gv0 + GLA_WIDTH
    aq = w_in[:, :ATTN_WIDTH].astype(BF16).reshape(rows, 2, ATTN_HEADS // 2, ATTN_HEAD_DIM)
    aq = aq.transpose(0, 2, 1, 3).reshape(rows, ATTN_WIDTH)
    glr = jnp.pad(w_in[:, glr0:glr0 + GLA_GATE_RANK].astype(BF16), ((0, 0), (0, LANES - GLA_GATE_RANK)))
    return jnp.concatenate([aq, w_in[:, ATTN_WIDTH:gv0].astype(BF16), glr, w_in[:, gv0:glr0].astype(BF16),
                            w_in[:, glr0 + GLA_GATE_RANK:].astype(BF16)], axis=-1)


def _resident(shape):
    return pl.BlockSpec(shape, lambda *_: (0,) * len(shape), pipeline_mode=pl.Buffered(1))


def _layer(layer, h, p, ffn1_norm, ffn1_w_gate, ffn1_w_up, ffn1_w_down, mix_norm, w_in, q_norm, k_norm, attn_sinks,
           gla_w_gate_up, gla_b_gate, gla_out_norm, w_out, ffn2_norm, ffn2_w_gate, ffn2_w_up, ffn2_w_down,
           ple_gate_norm, ple_w_gate, ple_w_proj, ple_norm):
    batch, seq, d = h.shape
    tokens = batch * seq
    x2 = h.reshape(tokens, d)
    rows = lambda width: pl.BlockSpec((ROW_TILE, width), lambda i: (i, 0))
    vec = lambda v: v.reshape(1, -1).astype(F32)

    assert seq == BLOCKS_PER_SEQ * ROW_TILE
    nblk = tokens // ROW_TILE
    cos_t, sin_t = _rope_tables(seq)
    tri, mfine, lid2 = _gla_constants()
    wup = jnp.pad(gla_w_gate_up, ((0, LANES - GLA_GATE_RANK), (0, 0))).astype(BF16)
    ffn_rows = pl.BlockSpec((ROW_TILE, d), lambda t: (jnp.minimum(t, nblk - 1), 0))
    mix_rows = pl.BlockSpec((ROW_TILE, d), lambda t: (jnp.maximum(t - 1, 0), 0))
    rope_rows = pl.BlockSpec((ROW_TILE, LANES), lambda t: (jnp.maximum(t - 1, 0) % BLOCKS_PER_SEQ, 0))
    h1, mix = pl.pallas_call(
        _ffn1_mixer_kernel,
        grid=(nblk + 1,),
        in_specs=[pl.BlockSpec(memory_space=pltpu.SMEM),
                  ffn_rows, _resident((1, d)), _resident((d, D_FF)), _resident((d, D_FF)), _resident((D_FF, d)),
                  _resident((1, d)), _resident((d, Z_WIDTH)),
                  rope_rows, rope_rows, _resident((1, LANES)), _resident((1, LANES)),
                  _resident((LANES, GLA_QK_WIDTH)), _resident((1, GLA_QK_WIDTH)), _resident((1, GLA_DV)),
                  _resident((TILE, 3 * TILE)), _resident((FINE_LEVELS * TILE, 2 * TILE)),
                  _resident((2 * TILE, TILE))],
        out_specs=[ffn_rows, mix_rows],
        out_shape=[jax.ShapeDtypeStruct((tokens, d), F32), jax.ShapeDtypeStruct((tokens, d), BF16)],
        scratch_shapes=[pltpu.VMEM((ROW_TILE, D_MODEL), BF16), pltpu.VMEM((ROW_TILE, Z_WIDTH), F32),
                        pltpu.VMEM((TILE, LANES), BF16), pltpu.VMEM((TILE, LANES), BF16),
                        pltpu.VMEM((GLA_QK_WIDTH, GLA_DV), F32)],
        compiler_params=pltpu.CompilerParams(dimension_semantics=("arbitrary",), vmem_limit_bytes=VMEM_LIMIT),
        name="ffn1_mixer",
    )(attn_sinks.astype(F32), x2, vec(ffn1_norm), ffn1_w_gate.astype(BF16), ffn1_w_up.astype(BF16),
      ffn1_w_down.astype(BF16), vec(mix_norm), _prep_w_in(w_in),
      cos_t, sin_t, vec(jnp.tile(q_norm, 2)), vec(jnp.tile(k_norm, 2)), wup, vec(gla_b_gate), vec(gla_out_norm),
      tri, mfine, lid2)

    wo_attn = w_out[:ATTN_WIDTH].astype(BF16).reshape(2, ATTN_HEADS // 2, ATTN_HEAD_DIM, d)
    wo = jnp.concatenate([wo_attn.transpose(1, 0, 2, 3).reshape(ATTN_WIDTH, d), w_out[ATTN_WIDTH:].astype(BF16)],
                         axis=0)
    out = pl.pallas_call(
        _outproj_ffn2_ple_kernel,
        grid=(tokens // ROW_TILE,),
        in_specs=[rows(d), rows(d), pl.BlockSpec((None, ROW_TILE, PLE_DIM), lambda r: (layer, r, 0)),
                  _resident((d, d)), _resident((1, d)), _resident((d, D_FF)),
                  _resident((d, D_FF)), _resident((D_FF, d)), _resident((1, d)), _resident((d, d)),
                  _resident((PLE_DIM, d)), _resident((1, d))],
        out_specs=rows(d),
        out_shape=jax.ShapeDtypeStruct((tokens, d), F32),
        compiler_params=pltpu.CompilerParams(dimension_semantics=("arbitrary",), vmem_limit_bytes=VMEM_LIMIT),
        name="outproj_ffn2_ple",
    )(mix, h1, p.reshape(p.shape[0], tokens, PLE_DIM), wo, vec(ffn2_norm), ffn2_w_gate.astype(BF16),
      ffn2_w_up.astype(BF16), ffn2_w_down.astype(BF16), vec(ple_gate_norm), ple_w_gate.astype(BF16),
      ple_w_proj.astype(BF16), vec(ple_norm))
    return out.reshape(batch, seq, d)


@jax.jit
def kernel(x, p, ffn1_norm, ffn1_w_gate, ffn1_w_up, ffn1_w_down, mix_norm, w_in, q_norm, k_norm, attn_sinks,
           gla_w_gate_up, gla_b_gate, gla_out_norm, w_out, ffn2_norm, ffn2_w_gate, ffn2_w_up, ffn2_w_down,
           ple_gate_norm, ple_w_gate, ple_w_proj, ple_norm):
    h = x
    for i in range(p.shape[0]):
        h = _layer(i, h, p, ffn1_norm[i], ffn1_w_gate[i], ffn1_w_up[i], ffn1_w_down[i], mix_norm[i], w_in[i],
                   q_norm[i], k_norm[i], attn_sinks[i], gla_w_gate_up[i], gla_b_gate[i], gla_out_norm[i], w_out[i],
                   ffn2_norm[i], ffn2_w_gate[i], ffn2_w_up[i], ffn2_w_down[i], ple_gate_norm[i], ple_w_gate[i],
                   ple_w_proj[i], ple_norm[i])
    return h
```
